```python
import math
import jax
import jax.numpy as jnp
from jax import lax
import numpy as np

D_MODEL = 2048
BATCH = 1
SEQ = 8192
DEPTH = 2
DEC_BATCH = 32
DEC_SEQ = 8
PAST_LEN = 8192
PAGE_SIZE = 128

N_AB = (DEPTH + 1) // 2
N_CD = DEPTH // 2

FOX_HEADS = 8
FOX_HEAD_DIM = 128
FOX_WIDTH = FOX_HEADS * FOX_HEAD_DIM
FOX_BIAS_LO = 3.0
FOX_BIAS_HI = 9.0

SSD_HEADS = 32
SSD_HEAD_DIM = 64
SSD_INNER = SSD_HEADS * SSD_HEAD_DIM
SSD_GROUPS = 4
SSD_STATE = 128
SSD_CONV = 4
SSD_CONV_CH = SSD_INNER + 2 * SSD_GROUPS * SSD_STATE
SSD_CHUNK = 128

HG_HEADS = 8
HG_KEY_DIM = 128
HG_VAL_DIM = 128
HG_KEY_WIDTH = HG_HEADS * HG_KEY_DIM
HG_VAL_WIDTH = HG_HEADS * HG_VAL_DIM
HG_CHUNK = 64

SB_HEADS = 8
SB_HEAD_DIM = 128
SB_WIDTH = SB_HEADS * SB_HEAD_DIM

MEM_LEN = 256
MEM_HEADS = 4
MEM_HEAD_DIM = 128
MEM_WIDTH = MEM_HEADS * MEM_HEAD_DIM

D_FF = 5632
Q_BLOCK = 128
NORM_EPS = 1e-6
NEG_INF = -1e30

AB_IN_SIZES = (FOX_WIDTH, FOX_WIDTH, FOX_WIDTH, FOX_HEADS, SSD_INNER, SSD_CONV_CH, SSD_HEADS)
AB_IN = sum(AB_IN_SIZES)
AB_OUT = FOX_WIDTH + SSD_INNER
CD_IN_SIZES = (HG_KEY_WIDTH, HG_KEY_WIDTH, HG_VAL_WIDTH, HG_VAL_WIDTH, SB_WIDTH, SB_WIDTH, SB_WIDTH)
CD_IN = sum(CD_IN_SIZES)
CD_OUT = HG_VAL_WIDTH + SB_WIDTH

kernel_name = "fox_ssd_hgrn2_stickbreaking_macaron_decoder_step"


def split_cols(a, sizes):
    return jnp.split(a, np.cumsum(sizes)[:-1].tolist(), axis=-1)


def rmsnorm(x, g):
    xf = x.astype(jnp.float32)
    y = xf * lax.rsqrt(jnp.mean(xf * xf, axis=-1, keepdims=True) + NORM_EPS)
    return (y * g.astype(jnp.float32)).astype(x.dtype)


def swiglu(x, w_gate, w_up, w_down):
    return (jax.nn.silu(x @ w_gate) * (x @ w_up)) @ w_down


def pad_time(a, pad):
    return jnp.pad(a, [(0, 0), (0, pad)] + [(0, 0)] * (a.ndim - 2))


def to_chunks(a, n, size):
    a = a.reshape((a.shape[0], n, size) + a.shape[2:])
    return jnp.moveaxis(a, 1, 0)


def from_chunks(a, t):
    a = jnp.moveaxis(a, 0, 1)
    return a.reshape((a.shape[0], a.shape[1] * a.shape[2]) + a.shape[3:])[:, :t]


def sweep_queries(block_fn, q_pos, *q_inputs):
    tq = q_pos.shape[0]
    blk = min(Q_BLOCK, tq)
    n_blk = -(-tq // blk)
    pad = n_blk * blk - tq
    pos = jnp.pad(q_pos, (0, pad), mode="edge").reshape(n_blk, blk)
    blocks = tuple(to_chunks(pad_time(a, pad), n_blk, blk) for a in q_inputs)
    out = lax.map(block_fn, (pos,) + blocks)
    return from_chunks(out, tq)


def forgetting_attention(q, k, v, q_cum, k_cum, q_pos, k_pos):
    scale = q.shape[-1] ** -0.5
    k_cum_t = jnp.swapaxes(k_cum.astype(jnp.float32), 1, 2)

    def block(args):
        pos, qb, qcb = args
        s = jnp.einsum("bqhd,bkhd->bhqk", qb, k).astype(jnp.float32) * scale
        s = s + jnp.swapaxes(qcb.astype(jnp.float32), 1, 2)[..., None] - k_cum_t[:, :, None, :]
        s = jnp.where(k_pos[None, :] <= pos[:, None], s, NEG_INF)
        p = jax.nn.softmax(s, axis=-1).astype(v.dtype)
        return jnp.einsum("bhqk,bkhd->bqhd", p, v)

    return sweep_queries(block, q_pos, q, q_cum)


def stick_breaking_attention(q, k, v, q_pos, k_pos):
    scale = q.shape[-1] ** -0.5

    def block(args):
        pos, qb = args
        z = jnp.einsum("bqhd,bkhd->bhqk", qb, k).astype(jnp.float32) * scale
        valid = k_pos[None, :] < pos[:, None]
        log_keep = jnp.where(valid, jax.nn.log_sigmoid(-z), 0.0)
        between = lax.cumsum(log_keep, axis=3, reverse=True) - log_keep
        w = jnp.where(valid, jnp.exp(jax.nn.log_sigmoid(z) + between), 0.0).astype(v.dtype)
        return jnp.einsum("bhqk,bkhd->bqhd", w, v)

    return sweep_queries(block, q_pos, q)


def memory_attention(h, mem_k, mem_v, w_q, w_o):
    b, t, _ = h.shape
    q = (h @ w_q).reshape(b, t, MEM_HEADS, MEM_HEAD_DIM)
    s = jnp.einsum("bthd,bmhd->bhtm", q, mem_k).astype(jnp.float32) * MEM_HEAD_DIM ** -0.5
    p = jax.nn.softmax(s, axis=-1).astype(mem_v.dtype)
    o = jnp.einsum("bhtm,bmhd->bthd", p, mem_v).reshape(b, t, MEM_WIDTH)
    return o @ w_o


def causal_depthwise_conv(u, prev, w, bias):
    full = jnp.concatenate([prev.astype(u.dtype), u], axis=1)
    out = lax.conv_general_dilated(full, w[:, None, :].astype(u.dtype), window_strides=(1,), padding="VALID",
                                   dimension_numbers=("NWC", "WIO", "NWC"), feature_group_count=u.shape[-1])
    return out + bias.astype(u.dtype), full[:, full.shape[1] - (SSD_CONV - 1):]


def ssd_chunk_scan(x, dt, a, b_in, c_in, h0):
    bsz, t = x.shape[:2]
    size = min(SSD_CHUNK, t)
    n = -(-t // size)
    pad = n * size - t
    xs, dts, bs, cs = (to_chunks(pad_time(u, pad), n, size) for u in (x, dt, b_in, c_in))
    hpg = SSD_HEADS // SSD_GROUPS
    causal = jnp.tril(jnp.ones((size, size), bool))[None, :, :, None, None]
    a_g = a.reshape(SSD_GROUPS, hpg)

    def step(h, inp):
        xc, dtc, bc, cc = inp
        xg = xc.reshape(bsz, size, SSD_GROUPS, hpg, SSD_HEAD_DIM)
        dtg = dtc.reshape(bsz, size, SSD_GROUPS, hpg)
        cum = jnp.cumsum(dtg * a_g, axis=1)
        seg = cum[:, :, None] - cum[:, None, :]
        decay = jnp.where(causal, jnp.exp(jnp.where(causal, seg, 0.0)), 0.0)
        cb = jnp.einsum("blgn,bsgn->blsg", cc, bc)
        y = jnp.einsum("blsgj,bsgjp->blgjp", cb[..., None] * decay * dtg[:, None], xg)
        hg = h.reshape(bsz, SSD_GROUPS, hpg, SSD_HEAD_DIM, SSD_STATE)
        y = y + jnp.einsum("blgn,bgjpn->blgjp", cc, hg) * jnp.exp(cum)[..., None]
        to_end = jnp.exp(cum[:, -1:] - cum) * dtg
        h_new = jnp.exp(cum[:, -1])[..., None, None] * hg + jnp.einsum("blgj,blgjp,blgn->bgjpn", to_end, xg, bc)
        return h_new.reshape(h.shape), y.reshape(bsz, size, SSD_HEADS, SSD_HEAD_DIM)

    h_final, ys = lax.scan(step, h0, (xs, dts, bs, cs))
    return from_chunks(ys, t), h_final


def hgrn2_chunk_scan(q, k, v, log_f, s0):
    bsz, t = q.shape[:2]
    size = min(HG_CHUNK, t)
    n = -(-t // size)
    pad = n * size - t
    qs, ks, vs, fs = (to_chunks(pad_time(u, pad), n, size) for u in (q, k, v, log_f))
    causal = jnp.tril(jnp.ones((size, size), bool))[None, :, :, None, None]

    def step(s, inp):
        qc, kc, vc, lfc = inp
        cum = jnp.cumsum(lfc, axis=1)
        rel = cum[:, :, None] - cum[:, None, :]
        decay = jnp.where(causal, jnp.exp(jnp.where(causal, rel, 0.0)), 0.0)
        att = jnp.einsum("blhk,blshk,bshk->bhls", qc, decay, kc)
        o = jnp.einsum("bhls,bshv->blhv", att, vc)
        o = o + jnp.einsum("blhk,bhkv->blhv", qc * jnp.exp(cum), s)
        to_end = jnp.exp(cum[:, -1:] - cum) * kc
        s_new = jnp.exp(cum[:, -1])[..., None] * s + jnp.einsum("blhk,blhv->bhkv", to_end, vc)
        return s_new, o

    s_final, os_ = lax.scan(step, s0, (qs, ks, vs, fs))
    return from_chunks(os_, t), s_final


def ab_mixer(h, fox_k_past, fox_v_past, fox_lf_past, conv_prev, ssm_prev,
             w_in, f_bias, conv_w, conv_b, dt_bias, a_log, d_skip, ssd_norm, w_out):
    f32 = jnp.float32
    bsz, t, _ = h.shape
    past = fox_k_past.shape[1]
    q, k, v, f_logit, z, xbc, dt_raw = split_cols(h @ w_in, AB_IN_SIZES)
    hs = (bsz, t, FOX_HEADS, FOX_HEAD_DIM)
    q, k, v = q.reshape(hs), k.reshape(hs), v.reshape(hs)
    log_f = jax.nn.log_sigmoid(f_logit.astype(f32) + f_bias.astype(f32))
    cum_new = jnp.cumsum(log_f, axis=1)
    lf_past = fox_lf_past.astype(f32)
    after_past = lax.cumsum(lf_past, axis=1, reverse=True) - lf_past
    k_cum = jnp.concatenate([-after_past, cum_new], axis=1)
    k_all = jnp.concatenate([fox_k_past.astype(k.dtype), k], axis=1)
    v_all = jnp.concatenate([fox_v_past.astype(v.dtype), v], axis=1)
    q_pos = past + jnp.arange(t)
    k_pos = jnp.arange(past + t)
    o_fox = forgetting_attention(q, k_all, v_all, cum_new, k_cum, q_pos, k_pos).reshape(bsz, t, FOX_WIDTH)
    xbc, conv_new = causal_depthwise_conv(xbc, conv_prev, conv_w, conv_b)
    xbc = jax.nn.silu(xbc)
    xs, b_in, c_in = split_cols(xbc, (SSD_INNER, SSD_GROUPS * SSD_STATE, SSD_GROUPS * SSD_STATE))
    dt = jax.nn.softplus(dt_raw.astype(f32) + dt_bias.astype(f32))
    a = -jnp.exp(a_log.astype(f32))
    xs4 = xs.reshape(bsz, t, SSD_HEADS, SSD_HEAD_DIM).astype(f32)
    gs = (bsz, t, SSD_GROUPS, SSD_STATE)
    y, ssm_new = ssd_chunk_scan(xs4, dt, a, b_in.reshape(gs).astype(f32), c_in.reshape(gs).astype(f32),
                                ssm_prev.astype(f32))
    y = y + d_skip.astype(f32)[:, None] * xs4
    y = y.reshape(bsz, t, SSD_INNER) * jax.nn.silu(z.astype(f32))
    y = rmsnorm(y.reshape(bsz, t, SSD_GROUPS, SSD_INNER // SSD_GROUPS),
                ssd_norm.reshape(SSD_GROUPS, SSD_INNER // SSD_GROUPS)).reshape(bsz, t, SSD_INNER)
    out = jnp.concatenate([o_fox, y.astype(h.dtype)], axis=-1) @ w_out
    return out, (k, v, log_f.astype(h.dtype), conv_new, ssm_new.astype(h.dtype))


def cd_mixer(h, hg_prev, sb_k_past, sb_v_past, lower_bound, w_in, hg_norm, w_out):
    f32 = jnp.float32
    bsz, t, _ = h.shape
    past = sb_k_past.shape[1]
    hq, hf, hi, hgate, sq, sk, sv = split_cols(h @ w_in, CD_IN_SIZES)
    f_logit = hf.astype(f32)
    log_f = jnp.logaddexp(jnp.log(lower_bound), jnp.log1p(-lower_bound) + jax.nn.log_sigmoid(f_logit))
    key = (1.0 - lower_bound) * jax.nn.sigmoid(-f_logit)
    ks4 = (bsz, t, HG_HEADS, HG_KEY_DIM)
    o, hg_new = hgrn2_chunk_scan(jax.nn.silu(hq.astype(f32)).reshape(ks4), key.reshape(ks4),
                                 hi.astype(f32).reshape(bsz, t, HG_HEADS, HG_VAL_DIM), log_f.reshape(ks4),
                                 hg_prev.astype(f32))
    o = rmsnorm(o, hg_norm.reshape(HG_HEADS, HG_VAL_DIM)).reshape(bsz, t, HG_VAL_WIDTH) * jax.nn.silu(hgate.astype(f32))
    hs = (bsz, t, SB_HEADS, SB_HEAD_DIM)
    sq, sk, sv = sq.reshape(hs), sk.reshape(hs), sv.reshape(hs)
    k_all = jnp.concatenate([sb_k_past.astype(sk.dtype), sk], axis=1)
    v_all = jnp.concatenate([sb_v_past.astype(sv.dtype), sv], axis=1)
    q_pos = past + jnp.arange(t)
    k_pos = jnp.arange(past + t)
    o_sb = stick_breaking_attention(sq, k_all, v_all, q_pos, k_pos).reshape(bsz, t, SB_WIDTH)
    out = jnp.concatenate([o.astype(h.dtype), o_sb], axis=-1) @ w_out
    return out, (hg_new.astype(h.dtype), sk, sv)


def run_trunk(x, mem_k, mem_v, fox_k_past, fox_v_past, fox_lf_past, conv_prev, ssm_prev,
              hg_prev, sb_k_past, sb_v_past, lower_bounds, p):
    fk, fv, fl, cv, ss, hgs, sbk, sbv = [], [], [], [], [], [], [], []
    for layer in range(DEPTH):
        i = layer // 2
        x = x + 0.5 * swiglu(rmsnorm(x, p["norm_ffn1"][layer]), p["ffn1_gate"][layer],
                             p["ffn1_up"][layer], p["ffn1_down"][layer])
        h = rmsnorm(x, p["norm_mix"][layer])
        if layer % 2 == 0:
            mix, (k, v, lf, c_new, s_new) = ab_mixer(
                h, fox_k_past[i], fox_v_past[i], fox_lf_past[i], conv_prev[i], ssm_prev[i],
                p["ab_w_in"][i], p["ab_fox_fbias"][i], p["ab_conv_w"][i], p["ab_conv_b"][i],
                p["ab_dt_bias"][i], p["ab_A_log"][i], p["ab_D"][i], p["ab_ssd_norm"][i], p["ab_w_out"][i])
            fk.append(k); fv.append(v); fl.append(lf); cv.append(c_new); ss.append(s_new)
        else:
            mix, (hg_new, k, v) = cd_mixer(
                h, hg_prev[i], sb_k_past[i], sb_v_past[i], lower_bounds[layer],
                p["cd_w_in"][i], p["cd_hg_norm"][i], p["cd_w_out"][i])
            hgs.append(hg_new); sbk.append(k); sbv.append(v)
        x = x + mix
        x = x + memory_attention(rmsnorm(x, p["norm_mem"][layer]), mem_k[layer], mem_v[layer],
                                 p["mem_wq"][layer], p["mem_wo"][layer])
        x = x + 0.5 * swiglu(rmsnorm(x, p["norm_ffn2"][layer]), p["ffn2_gate"][layer],
                             p["ffn2_up"][layer], p["ffn2_down"][layer])
    y = rmsnorm(x, p["norm_final"])
    return (y, jnp.stack(fk), jnp.stack(fv), jnp.stack(fl), jnp.stack(cv), jnp.stack(ss),
            jnp.stack(hgs), jnp.stack(sbk), jnp.stack(sbv))


def gather_pages(cache, page_table):
    g = cache[:, page_table]
    return g.reshape((g.shape[0], g.shape[1], g.shape[2] * g.shape[3]) + g.shape[4:])


def setup_inputs(seed: int = 0) -> dict:
    key = jax.random.key(seed)
    ks = iter(jax.random.split(key, 64))
    f32 = jnp.float32

    def nrm(shape, scale=1.0):
        return scale * jax.random.normal(next(ks), shape, f32)

    def gain(shape):
        return 1.0 + 0.02 * jax.random.normal(next(ks), shape, f32)

    n_pages = PAST_LEN // PAGE_SIZE
    n_pool = (5 * DEC_BATCH * n_pages + 3) // 4
    d = D_MODEL
    page_table = jax.random.permutation(next(ks), n_pool)[: DEC_BATCH * n_pages].reshape(DEC_BATCH, n_pages).astype(jnp.int32)
    dt0 = jnp.exp(jax.random.uniform(next(ks), (N_AB, SSD_HEADS), f32, math.log(1e-3), math.log(1e-1)))
    fox_head_bias = jnp.linspace(FOX_BIAS_LO, FOX_BIAS_HI, FOX_HEADS, dtype=f32)
    return {
        "x_prompt": nrm((BATCH, SEQ, d)),
        "x_sample": nrm((DEC_BATCH, DEC_SEQ, d)),
        "cache_fox_k": nrm((N_AB, n_pool, PAGE_SIZE, FOX_HEADS, FOX_HEAD_DIM)),
        "cache_fox_v": nrm((N_AB, n_pool, PAGE_SIZE, FOX_HEADS, FOX_HEAD_DIM)),
        "cache_fox_logf": jax.nn.log_sigmoid(fox_head_bias + nrm((N_AB, n_pool, PAGE_SIZE, FOX_HEADS))),
        "state_ssd_conv": nrm((N_AB, DEC_BATCH, SSD_CONV - 1, SSD_CONV_CH)),
        "state_ssd": nrm((N_AB, DEC_BATCH, SSD_HEADS, SSD_HEAD_DIM, SSD_STATE), 0.1),
        "state_hgrn": nrm((N_CD, DEC_BATCH, HG_HEADS, HG_KEY_DIM, HG_VAL_DIM), 0.5),
        "cache_sb_k": nrm((N_CD, n_pool, PAGE_SIZE, SB_HEADS, SB_HEAD_DIM)),
        "cache_sb_v": nrm((N_CD, n_pool, PAGE_SIZE, SB_HEADS, SB_HEAD_DIM)),
        "cache_mem_k": nrm((DEPTH, DEC_BATCH, MEM_LEN, MEM_HEADS, MEM_HEAD_DIM)),
        "cache_mem_v": nrm((DEPTH, DEC_BATCH, MEM_LEN, MEM_HEADS, MEM_HEAD_DIM)),
        "page_table": page_table,
        "mem_prompt": nrm((BATCH, MEM_LEN, d)),
        "norm_ffn1": gain((DEPTH, d)),
        "ffn1_gate": nrm((DEPTH, d, D_FF), d ** -0.5),
        "ffn1_up": nrm((DEPTH, d, D_FF), d ** -0.5),
        "ffn1_down": nrm((DEPTH, D_FF, d), D_FF ** -0.5),
        "norm_mix": gain((DEPTH, d)),
        "ab_w_in": nrm((N_AB, d, AB_IN), d ** -0.5),
        "ab_fox_fbias": fox_head_bias + 0.3 * nrm((N_AB, FOX_HEADS)),
        "ab_conv_w": nrm((N_AB, SSD_CONV, SSD_CONV_CH), SSD_CONV ** -0.5),
        "ab_conv_b": nrm((N_AB, SSD_CONV_CH), 0.01),
        "ab_dt_bias": dt0 + jnp.log(-jnp.expm1(-dt0)),
        "ab_A_log": jnp.log(jax.random.uniform(next(ks), (N_AB, SSD_HEADS), f32, 1.0, 16.0)),
        "ab_D": gain((N_AB, SSD_HEADS)),
        "ab_ssd_norm": gain((N_AB, SSD_INNER)),
        "ab_w_out": nrm((N_AB, AB_OUT, d), AB_OUT ** -0.5),
        "cd_w_in": nrm((N_CD, d, CD_IN), d ** -0.5),
        "hg_lower_bound": nrm((DEPTH, HG_KEY_WIDTH), 0.1),
        "cd_hg_norm": gain((N_CD, HG_VAL_WIDTH)),
        "cd_w_out": nrm((N_CD, CD_OUT, d), CD_OUT ** -0.5),
        "norm_mem": gain((DEPTH, d)),
        "norm_memkv": gain((DEPTH, d)),
        "mem_wq": nrm((DEPTH, d, MEM_WIDTH), d ** -0.5),
        "mem_wk": nrm((DEPTH, d, MEM_WIDTH), d ** -0.5),
        "mem_wv": nrm((DEPTH, d, MEM_WIDTH), d ** -0.5),
        "mem_wo": nrm((DEPTH, MEM_WIDTH, d), MEM_WIDTH ** -0.5),
        "norm_ffn2": gain((DEPTH, d)),
        "ffn2_gate": nrm((DEPTH, d, D_FF), d ** -0.5),
        "ffn2_up": nrm((DEPTH, d, D_FF), d ** -0.5),
        "ffn2_down": nrm((DEPTH, D_FF, d), D_FF ** -0.5),
        "norm_final": gain((d,)),
    }


def reference(x_prompt, x_sample, cache_fox_k, cache_fox_v, cache_fox_logf, state_ssd_conv, state_ssd,
              state_hgrn, cache_sb_k, cache_sb_v, cache_mem_k, cache_mem_v, page_table, mem_prompt,
              norm_ffn1, ffn1_gate, ffn1_up, ffn1_down, norm_mix, ab_w_in, ab_fox_fbias, ab_conv_w, ab_conv_b,
              ab_dt_bias, ab_A_log, ab_D, ab_ssd_norm, ab_w_out, cd_w_in, hg_lower_bound, cd_hg_norm, cd_w_out,
              norm_mem, norm_memkv, mem_wq, mem_wk, mem_wv, mem_wo, norm_ffn2, ffn2_gate, ffn2_up, ffn2_down,
              norm_final):
    p = dict(norm_ffn1=norm_ffn1, ffn1_gate=ffn1_gate, ffn1_up=ffn1_up, ffn1_down=ffn1_down, norm_mix=norm_mix,
             ab_w_in=ab_w_in, ab_fox_fbias=ab_fox_fbias, ab_conv_w=ab_conv_w, ab_conv_b=ab_conv_b,
             ab_dt_bias=ab_dt_bias, ab_A_log=ab_A_log, ab_D=ab_D, ab_ssd_norm=ab_ssd_norm, ab_w_out=ab_w_out,
             cd_w_in=cd_w_in, cd_hg_norm=cd_hg_norm, cd_w_out=cd_w_out, norm_mem=norm_mem, mem_wq=mem_wq,
             mem_wo=mem_wo, norm_ffn2=norm_ffn2, ffn2_gate=ffn2_gate, ffn2_up=ffn2_up, ffn2_down=ffn2_down,
             norm_final=norm_final)
    probs = jax.nn.softmax(hg_lower_bound.astype(jnp.float32), axis=0)
    lower_bounds = jnp.cumsum(probs, axis=0) - probs[0]

    bp = x_prompt.shape[0]
    dtp = x_prompt.dtype
    mem_h = rmsnorm(mem_prompt[None], norm_memkv[:, None, None, :])
    mem_k_p = jnp.einsum("lbmd,lde->lbme", mem_h, mem_wk).reshape(DEPTH, bp, MEM_LEN, MEM_HEADS, MEM_HEAD_DIM)
    mem_v_p = jnp.einsum("lbmd,lde->lbme", mem_h, mem_wv).reshape(DEPTH, bp, MEM_LEN, MEM_HEADS, MEM_HEAD_DIM)
    y_prompt, fox_k_p, fox_v_p, fox_logf_p, ssd_conv_p, ssd_state_p, hgrn_state_p, sb_k_p, sb_v_p = run_trunk(
        x_prompt, mem_k_p, mem_v_p,
        jnp.zeros((N_AB, bp, 0, FOX_HEADS, FOX_HEAD_DIM), dtp),
        jnp.zeros((N_AB, bp, 0, FOX_HEADS, FOX_HEAD_DIM), dtp),
        jnp.zeros((N_AB, bp, 0, FOX_HEADS), dtp),
        jnp.zeros((N_AB, bp, SSD_CONV - 1, SSD_CONV_CH), dtp),
        jnp.zeros((N_AB, bp, SSD_HEADS, SSD_HEAD_DIM, SSD_STATE), dtp),
        jnp.zeros((N_CD, bp, HG_HEADS, HG_KEY_DIM, HG_VAL_DIM), dtp),
        jnp.zeros((N_CD, bp, 0, SB_HEADS, SB_HEAD_DIM), dtp),
        jnp.zeros((N_CD, bp, 0, SB_HEADS, SB_HEAD_DIM), dtp),
        lower_bounds, p)

    y_sample, fox_k_s, fox_v_s, fox_logf_s, ssd_conv_s, ssd_state_s, hgrn_state_s, sb_k_s, sb_v_s = run_trunk(
        x_sample, cache_mem_k, cache_mem_v,
        gather_pages(cache_fox_k, page_table), gather_pages(cache_fox_v, page_table),
        gather_pages(cache_fox_logf, page_table),
        state_ssd_conv, state_ssd, state_hgrn,
        gather_pages(cache_sb_k, page_table), gather_pages(cache_sb_v, page_table),
        lower_bounds, p)

    return (y_prompt, y_sample,
            fox_k_p, fox_v_p, fox_logf_p, ssd_conv_p, ssd_state_p, hgrn_state_p, sb_k_p, sb_v_p, mem_k_p, mem_v_p,
            fox_k_s, fox_v_s, fox_logf_s, ssd_conv_s, ssd_state_s, hgrn_state_s, sb_k_s, sb_v_s)
```

```python
import functools
import math

import numpy as np
import jax
import jax.numpy as jnp
from jax import lax
from jax.experimental import pallas as pl
from jax.experimental.pallas import tpu as pltpu

f32 = jnp.float32
bf16 = jnp.bfloat16
HIGHEST = lax.Precision.HIGHEST

D_MODEL = 2048
D_FF = 5632
HEADS = 8
HEAD_DIM = 128
WIDTH = HEADS * HEAD_DIM
SSD_HEADS = 32
SSD_HEAD_DIM = 64
SSD_INNER = SSD_HEADS * SSD_HEAD_DIM
SSD_GROUPS = 4
SSD_STATE = 128
SSD_CONV = 4
SSD_CONV_CH = SSD_INNER + 2 * SSD_GROUPS * SSD_STATE
SSD_CHUNK = 128
HG_CHUNK = 64
HG_SUB = 16
MEM_HEADS = 4
MEM_WIDTH = MEM_HEADS * HEAD_DIM
PAGE = 128
NORM_EPS = 1e-6
NEG_INF = -1e30
ATTN_SCALE = HEAD_DIM ** -0.5

LANES = 128
SUBLANES = 8
VMEM_LIMIT_BYTES = 56 * 1024 * 1024
FDT_LANE0 = 8

NT_DIMS = (((1,), (1,)), ((), ()))
TN_DIMS = (((0,), (0,)), ((), ()))


def _params(sem):
    return pltpu.CompilerParams(dimension_semantics=sem, vmem_limit_bytes=VMEM_LIMIT_BYTES)


def _dot(a, b):
    return jnp.dot(a, b, preferred_element_type=f32)


def _dot_nt(a, b):
    return lax.dot_general(a, b, NT_DIMS, preferred_element_type=f32)


def _dot_tn(a, b):
    return lax.dot_general(a, b, TN_DIMS, preferred_element_type=f32)


def _split_dot(x, e):
    hi = x.astype(bf16)
    lo = (x - hi.astype(f32)).astype(bf16)
    return _dot(hi, e) + _dot(lo, e)


def _silu(x):
    return x * jax.nn.sigmoid(x)


def _log_sigmoid(x):
    return jnp.minimum(x, 0.0) - jnp.log(1.0 + jnp.exp(-jnp.abs(x)))


def _nmm_kernel(x_ref, g_ref, *refs, n_w, swiglu):
    w_refs, o_ref, xn_ref = refs[:n_w], refs[n_w], refs[n_w + 1]

    @pl.when(pl.program_id(1) == 0)
    def _():
        x = x_ref[...]
        ms = jnp.mean(x * x, axis=-1, keepdims=True)
        xn_ref[...] = (x * lax.rsqrt(ms + NORM_EPS) * g_ref[...]).astype(bf16)

    xn = xn_ref[...]
    if swiglu:
        g = _dot(xn, w_refs[0][...].astype(bf16))
        u = _dot(xn, w_refs[1][...].astype(bf16))
        o_ref[...] = (_silu(g) * u).astype(o_ref.dtype)
    else:
        o_ref[...] = _dot(xn, w_refs[0][...].astype(bf16)).astype(o_ref.dtype)


def _w_spec(w, lead, k, tn, off):
    if w.ndim == 3:
        return pl.BlockSpec((None, k, tn), lambda i, j: (lead, 0, j + off))
    return pl.BlockSpec((k, tn), lambda i, j: (0, j + off))


def norm_matmul(x, gain, ws, *, n, tn, tm, out_dtype=f32, swiglu=False, name="nmm"):
    m, k = x.shape
    assert m % tm == 0 and n % tn == 0
    in_specs = [pl.BlockSpec((tm, k), lambda i, j: (i, 0)), pl.BlockSpec((1, k), lambda i, j: (0, 0))]
    in_specs += [_w_spec(w, lead, k, tn, off) for (w, lead, off) in ws]
    return pl.pallas_call(
        functools.partial(_nmm_kernel, n_w=len(ws), swiglu=swiglu),
        out_shape=jax.ShapeDtypeStruct((m, n), out_dtype),
        grid=(m // tm, n // tn),
        in_specs=in_specs,
        out_specs=pl.BlockSpec((tm, tn), lambda i, j: (i, j)),
        scratch_shapes=[pltpu.VMEM((tm, k), bf16)],
        compiler_params=_params(("arbitrary", "arbitrary")),
        name=name,
    )(x, gain.reshape(1, k), *[w for (w, _, _) in ws])


def _mmr_kernel(a_ref, w_ref, r_ref, o_ref, acc_ref, *, nk, scale):
    k = pl.program_id(2)
    p = _dot(a_ref[...].astype(bf16), w_ref[...].astype(bf16))

    @pl.when(k == 0)
    def _():
        acc_ref[...] = p

    @pl.when(k > 0)
    def _():
        acc_ref[...] += p

    @pl.when(k == nk - 1)
    def _():
        o_ref[...] = r_ref[...] + scale * acc_ref[...]


def matmul_residual(a, w, lead, res, *, scale, tm, tn, tk, name="mmr"):
    m, kk = a.shape
    n = res.shape[1]
    assert m % tm == 0 and n % tn == 0 and kk % tk == 0
    nk = kk // tk
    if w.ndim == 3:
        w_spec = pl.BlockSpec((None, tk, tn), lambda i, j, k: (lead, k, j))
    else:
        w_spec = pl.BlockSpec((tk, tn), lambda i, j, k: (k, j))
    return pl.pallas_call(
        functools.partial(_mmr_kernel, nk=nk, scale=scale),
        out_shape=jax.ShapeDtypeStruct((m, n), f32),
        grid=(m // tm, n // tn, nk),
        in_specs=[pl.BlockSpec((tm, tk), lambda i, j, k: (i, k)), w_spec,
                  pl.BlockSpec((tm, tn), lambda i, j, k: (i, j))],
        out_specs=pl.BlockSpec((tm, tn), lambda i, j, k: (i, j)),
        scratch_shapes=[pltpu.VMEM((tm, tn), f32)],
        compiler_params=_params(("arbitrary", "arbitrary", "arbitrary")),
        name=name,
    )(a, w, res)


def _rmsnorm_kernel(x_ref, g_ref, o_ref):
    x = x_ref[...]
    ms = jnp.mean(x * x, axis=-1, keepdims=True)
    o_ref[...] = x * lax.rsqrt(ms + NORM_EPS) * g_ref[...]


def rmsnorm_rows(x, gain, *, tm):
    m, k = x.shape
    return pl.pallas_call(
        _rmsnorm_kernel,
        out_shape=jax.ShapeDtypeStruct((m, k), f32),
        grid=(m // tm,),
        in_specs=[pl.BlockSpec((tm, k), lambda i: (i, 0)), pl.BlockSpec((1, k), lambda i: (0, 0))],
        out_specs=pl.BlockSpec((tm, k), lambda i: (i, 0)),
        compiler_params=_params(("arbitrary",)),
        name="final_norm",
    )(x, gain.reshape(1, k))


def _fox_gate_kernel(fdt_ref, bias_ref, u_ref, lf_ref, ck_ref, carry_ref, *, tb, seg, carry):
    lf = _log_sigmoid(fdt_ref[...] + bias_ref[...])
    lf_ref[...] = lf[:, :HEADS]
    lft = lf.T[:HEADS, :]
    cum = jnp.dot(lft, u_ref[...], preferred_element_type=f32, precision=HIGHEST)
    if carry:
        @pl.when((pl.program_id(0) * tb) % seg == 0)
        def _():
            carry_ref[...] = jnp.zeros_like(carry_ref)
        cum = cum + carry_ref[:, 0:1]
        carry_ref[...] = jnp.broadcast_to(cum[:, tb - 1:tb], carry_ref.shape)
    ck_ref[...] = cum


def fox_gate(fdt, fbias, *, seg, tb):
    m = fdt.shape[0]
    assert m % tb == 0 and (seg % tb == 0 or tb % seg == 0)
    r = np.arange(tb)
    u = ((r[:, None] <= r[None, :]) & (r[:, None] // seg == r[None, :] // seg)).astype(np.float32)
    bias = jnp.zeros((1, LANES), f32).at[0, :HEADS].set(fbias)
    return pl.pallas_call(
        functools.partial(_fox_gate_kernel, tb=tb, seg=seg, carry=seg > tb),
        out_shape=(jax.ShapeDtypeStruct((m, HEADS), f32), jax.ShapeDtypeStruct((HEADS, m), f32)),
        grid=(m // tb,),
        in_specs=[pl.BlockSpec((tb, LANES), lambda i: (i, 0)), pl.BlockSpec((1, LANES), lambda i: (0, 0)),
                  pl.BlockSpec((tb, tb), lambda i: (0, 0))],
        out_specs=(pl.BlockSpec((tb, HEADS), lambda i: (i, 0)), pl.BlockSpec((HEADS, tb), lambda i: (0, i))),
        scratch_shapes=[pltpu.VMEM((HEADS, LANES), f32)],
        compiler_params=_params(("arbitrary",)),
        name="fox_gate",
    )(fdt, bias, jnp.asarray(u))


def _fox_flash_kernel(q_ref, k_ref, v_ref, ck_ref, o_ref, m_ref, l_ref, acc_ref, *, tq):
    i = pl.program_id(1)
    q = q_ref[...].astype(bf16)
    m_ref[...] = jnp.full(m_ref.shape, NEG_INF, f32)
    l_ref[...] = jnp.zeros(l_ref.shape, f32)
    acc_ref[...] = jnp.zeros(acc_ref.shape, f32)

    def step(j, masked):
        start = pl.multiple_of(j * tq, tq)
        kb = k_ref[pl.ds(start, tq), :].astype(bf16)
        vb = v_ref[pl.ds(start, tq), :].astype(bf16)
        s = _dot_nt(q, kb) * ATTN_SCALE - ck_ref[:, pl.ds(start, tq)]
        if masked:
            row = lax.broadcasted_iota(jnp.int32, (tq, tq), 0)
            col = lax.broadcasted_iota(jnp.int32, (tq, tq), 1)
            s = jnp.where(col <= row, s, NEG_INF)
        m_prev = m_ref[...]
        m_new = jnp.maximum(m_prev, jnp.max(s, axis=-1, keepdims=True))
        alpha = jnp.exp(m_prev - m_new)
        p = jnp.exp(s - m_new)
        l_ref[...] = alpha * l_ref[...] + jnp.sum(p, axis=-1, keepdims=True)
        acc_ref[...] = alpha * acc_ref[...] + _dot(p.astype(bf16), vb)
        m_ref[...] = m_new

    def body(j, c):
        step(j, False)
        return c

    lax.fori_loop(0, i, body, 0)
    step(i, True)
    o_ref[...] = (acc_ref[...] / l_ref[...]).astype(o_ref.dtype)


def fox_flash(qkv, ck, *, tq):
    t = qkv.shape[0]
    assert t % tq == 0
    return pl.pallas_call(
        functools.partial(_fox_flash_kernel, tq=tq),
        out_shape=jax.ShapeDtypeStruct((t, WIDTH), bf16),
        grid=(HEADS, t // tq),
        in_specs=[pl.BlockSpec((tq, HEAD_DIM), lambda h, i: (i, h)),
                  pl.BlockSpec((t, HEAD_DIM), lambda h, i: (0, HEADS + h)),
                  pl.BlockSpec((t, HEAD_DIM), lambda h, i: (0, 2 * HEADS + h)),
                  pl.BlockSpec((None, 1, t), lambda h, i: (h, 0, 0))],
        out_specs=pl.BlockSpec((tq, HEAD_DIM), lambda h, i: (i, h)),
        scratch_shapes=[pltpu.VMEM((tq, 1), f32), pltpu.VMEM((tq, 1), f32), pltpu.VMEM((tq, HEAD_DIM), f32)],
        compiler_params=_params(("arbitrary", "arbitrary")),
        name="fox_flash",
    )(qkv, qkv, qkv, ck.reshape(HEADS, 1, t))


def _sb_weights(z, valid, tri, r):
    ls = _log_sigmoid(z)
    lk = ls - z
    if valid is not None:
        lk = jnp.where(valid, lk, 0.0)
    incl = _split_dot(lk, tri)
    w = jnp.exp(ls + (incl - lk) + r)
    if valid is not None:
        w = jnp.where(valid, w, 0.0)
    return w, r + incl[:, 0:1]


def _sb_flash_kernel(q_ref, k_ref, v_ref, tri_ref, o_ref, r_ref, acc_ref, *, tq, tk):
    i = pl.program_id(1)
    q = q_ref[...].astype(bf16)
    r_ref[...] = jnp.zeros(r_ref.shape, f32)
    acc_ref[...] = jnp.zeros(acc_ref.shape, f32)
    nsub = tq // tk

    def step(jb, masked):
        start = pl.multiple_of(jb * tk, tk)
        kb = k_ref[pl.ds(start, tk), :].astype(bf16)
        vb = v_ref[pl.ds(start, tk), :].astype(bf16)
        z = _dot_nt(q, kb) * ATTN_SCALE
        valid = None
        if masked:
            row = lax.broadcasted_iota(jnp.int32, (tq, tk), 0) + i * tq
            col = lax.broadcasted_iota(jnp.int32, (tq, tk), 1) + jb * tk
            valid = col < row
        w, r_new = _sb_weights(z, valid, tri_ref[...], r_ref[...])
        acc_ref[...] += _dot(w.astype(bf16), vb)
        r_ref[...] = r_new

    for d in range(nsub):
        step(i * nsub + (nsub - 1 - d), True)

    def body(t, c):
        step(i * nsub - 1 - t, False)
        return c

    lax.fori_loop(0, i * nsub, body, 0)
    o_ref[...] = acc_ref[...].astype(o_ref.dtype)


def _tri_ge(n):
    r = np.arange(n)
    return jnp.asarray((r[:, None] >= r[None, :]).astype(np.float32), dtype=bf16)


def sb_flash(proj, col0, *, tq, tk):
    t = proj.shape[0]
    assert t % tq == 0 and tq % tk == 0
    return pl.pallas_call(
        functools.partial(_sb_flash_kernel, tq=tq, tk=tk),
        out_shape=jax.ShapeDtypeStruct((t, WIDTH), bf16),
        grid=(HEADS, t // tq),
        in_specs=[pl.BlockSpec((tq, HEAD_DIM), lambda h, i: (i, col0 + h)),
                  pl.BlockSpec((t, HEAD_DIM), lambda h, i: (0, col0 + HEADS + h)),
                  pl.BlockSpec((t, HEAD_DIM), lambda h, i: (0, col0 + 2 * HEADS + h)),
                  pl.BlockSpec((tk, tk), lambda h, i: (0, 0))],
        out_specs=pl.BlockSpec((tq, HEAD_DIM), lambda h, i: (i, h)),
        scratch_shapes=[pltpu.VMEM((tq, 1), f32), pltpu.VMEM((tq, HEAD_DIM), f32)],
        compiler_params=_params(("arbitrary", "arbitrary")),
        name="sb_flash",
    )(proj, proj, proj, _tri_ge(tk))


def _past_bias_kernel(pt_ref, lf_hbm, u_ref, o_ref, buf, sem, *, n_pages):
    b = pl.program_id(0)

    def copy(j):
        return pltpu.make_async_copy(lf_hbm.at[0, pt_ref[b, j]], buf.at[j], sem)

    def start(j, c):
        copy(j).start()
        return c

    def wait(j, c):
        copy(j).wait()
        return c

    lax.fori_loop(0, n_pages, start, 0)
    lax.fori_loop(0, n_pages, wait, 0)

    def body(t, carry):
        j = n_pages - 1 - t
        lf = buf[j]
        suf = jnp.dot(u_ref[...], lf, preferred_element_type=f32, precision=HIGHEST)
        o_ref[j] = suf + carry
        return carry + suf[0:1, :] + lf[0:1, :]

    lax.fori_loop(0, n_pages, body, jnp.zeros((1, HEADS), f32))


def fox_past_bias(cache_logf, page_table):
    nb, n_pages = page_table.shape
    r = np.arange(PAGE)
    u = jnp.asarray((r[None, :] > r[:, None]).astype(np.float32))
    return pl.pallas_call(
        functools.partial(_past_bias_kernel, n_pages=n_pages),
        out_shape=jax.ShapeDtypeStruct((nb, n_pages, PAGE, HEADS), f32),
        grid_spec=pltpu.PrefetchScalarGridSpec(
            num_scalar_prefetch=1,
            grid=(nb,),
            in_specs=[pl.BlockSpec(memory_space=pl.ANY), pl.BlockSpec((PAGE, PAGE), lambda b, pt: (0, 0))],
            out_specs=pl.BlockSpec((None, n_pages, PAGE, HEADS), lambda b, pt: (b, 0, 0, 0)),
            scratch_shapes=[pltpu.VMEM((n_pages, PAGE, HEADS), f32), pltpu.SemaphoreType.DMA(())],
        ),
        compiler_params=_params(("arbitrary",)),
        name="fox_past_bias",
    )(page_table, cache_logf, u)


def _decode_kernel(pt_ref, q_ref, kn_ref, vn_ref, *refs, mode, pp, n_new):
    k_refs, v_refs = refs[:pp], refs[pp:2 * pp]
    refs = refs[2 * pp:]
    if mode == "fox":
        bn_ref, ap_refs = refs[0], refs[1:1 + pp]
        o_ref, m_ref, l_ref, acc_ref = refs[1 + pp:]
    else:
        tri_ref = refs[0]
        o_ref, r_ref, acc_ref = refs[1:]
    j = pl.program_id(1)
    rows = HEADS * n_new

    def scores(get_k):
        zs = [_dot_nt(q_ref[h * n_new:(h + 1) * n_new, :].astype(bf16), get_k(h).astype(bf16)) for h in range(HEADS)]
        return jnp.concatenate(zs, axis=0) * ATTN_SCALE

    def values(w, get_v):
        os_ = [_dot(w[h * n_new:(h + 1) * n_new].astype(bf16), get_v(h).astype(bf16)) for h in range(HEADS)]
        return jnp.concatenate(os_, axis=0)

    def per_head_rows(x):
        return jnp.concatenate([jnp.broadcast_to(x[h:h + 1], (n_new, PAGE)) for h in range(HEADS)], axis=0)

    def block(get_k, get_v, bias, valid):
        z = scores(get_k)
        if mode == "fox":
            s = z + per_head_rows(bias)
            if valid is not None:
                s = jnp.where(valid, s, NEG_INF)
            m_prev = m_ref[...]
            m_new = jnp.maximum(m_prev, jnp.max(s, axis=-1, keepdims=True))
            alpha = jnp.exp(m_prev - m_new)
            p = jnp.exp(s - m_new)
            l_ref[...] = alpha * l_ref[...] + jnp.sum(p, axis=-1, keepdims=True)
            acc_ref[...] = alpha * acc_ref[...] + values(p, get_v)
            m_ref[...] = m_new
        else:
            w, r_new = _sb_weights(z, valid, tri_ref[...], r_ref[...])
            acc_ref[...] += values(w, get_v)
            r_ref[...] = r_new

    @pl.when(j == 0)
    def _():
        acc_ref[...] = jnp.zeros(acc_ref.shape, f32)
        if mode == "fox":
            m_ref[...] = jnp.full(m_ref.shape, NEG_INF, f32)
            l_ref[...] = jnp.zeros(l_ref.shape, f32)
        else:
            r_ref[...] = jnp.zeros(r_ref.shape, f32)
        tq = lax.broadcasted_iota(jnp.int32, (rows, PAGE), 0) % n_new
        sk = lax.broadcasted_iota(jnp.int32, (rows, PAGE), 1)
        valid = (sk <= tq) if mode == "fox" else (sk < tq)
        block(lambda h: kn_ref[h], lambda h: vn_ref[h], bn_ref[...] if mode == "fox" else None, valid)

    for p in range(pp - 1, -1, -1):
        block(lambda h, p=p: k_refs[p][:, h, :], lambda h, p=p: v_refs[p][:, h, :],
              ap_refs[p][...] if mode == "fox" else None, None)

    @pl.when(j == pl.num_programs(1) - 1)
    def _():
        if mode == "fox":
            o_ref[...] = acc_ref[...] / l_ref[...]
        else:
            o_ref[...] = acc_ref[...]


def decode_attention(mode, q, k_new, v_new, cache_k, cache_v, page_table, *, bias_new=None, bias_past=None, pp=4):
    nb, rows, _ = q.shape
    n_new = rows // HEADS
    n_pages = page_table.shape[1]
    assert n_pages % pp == 0
    nsteps = n_pages // pp

    def page_of(b, j, pt, p):
        return pt[b, (nsteps - 1 - j) * pp + p]

    kv_specs = [pl.BlockSpec((None, None, PAGE, HEADS, HEAD_DIM),
                             lambda b, j, pt, p=p: (0, page_of(b, j, pt, p), 0, 0, 0)) for p in range(pp)]
    in_specs = [pl.BlockSpec((None, rows, HEAD_DIM), lambda b, j, pt: (b, 0, 0)),
                pl.BlockSpec((None, HEADS, PAGE, HEAD_DIM), lambda b, j, pt: (b, 0, 0, 0)),
                pl.BlockSpec((None, HEADS, PAGE, HEAD_DIM), lambda b, j, pt: (b, 0, 0, 0))]
    in_specs += kv_specs + kv_specs
    args = [q, k_new, v_new] + [cache_k] * pp + [cache_v] * pp
    if mode == "fox":
        in_specs.append(pl.BlockSpec((None, HEADS, PAGE), lambda b, j, pt: (b, 0, 0)))
        in_specs += [pl.BlockSpec((None, None, HEADS, PAGE),
                                  lambda b, j, pt, p=p: (b, (nsteps - 1 - j) * pp + p, 0, 0)) for p in range(pp)]
        args += [bias_new] + [bias_past] * pp
        scratch = [pltpu.VMEM((rows, 1), f32), pltpu.VMEM((rows, 1), f32), pltpu.VMEM((rows, HEAD_DIM), f32)]
    else:
        in_specs.append(pl.BlockSpec((PAGE, PAGE), lambda b, j, pt: (0, 0)))
        args.append(_tri_ge(PAGE))
        scratch = [pltpu.VMEM((rows, 1), f32), pltpu.VMEM((rows, HEAD_DIM), f32)]
    return pl.pallas_call(
        functools.partial(_decode_kernel, mode=mode, pp=pp, n_new=n_new),
        out_shape=jax.ShapeDtypeStruct((nb, rows, HEAD_DIM), f32),
        grid_spec=pltpu.PrefetchScalarGridSpec(
            num_scalar_prefetch=1,
            grid=(nb, nsteps),
            in_specs=in_specs,
            out_specs=pl.BlockSpec((None, rows, HEAD_DIM), lambda b, j, pt: (b, 0, 0)),
            scratch_shapes=scratch,
        ),
        compiler_params=_params(("arbitrary", "arbitrary")),
        name=mode + "_decode",
    )(page_table, *args)


def _mem_attn_kernel(q_ref, k_ref, v_ref, o_ref):
    s = _dot_nt(q_ref[...].astype(bf16), k_ref[...].astype(bf16)) * ATTN_SCALE
    m = jnp.max(s, axis=-1, keepdims=True)
    p = jnp.exp(s - m)
    l = jnp.sum(p, axis=-1, keepdims=True)
    o_ref[...] = (_dot(p.astype(bf16), v_ref[...].astype(bf16)) / l).astype(o_ref.dtype)


def mem_attention(q, mem_k, mem_v, layer, *, tq):
    m = q.shape[0]
    nb, mem_len = mem_k.shape[1], mem_k.shape[2]
    per_b = m // nb
    assert per_b % tq == 0
    nq = per_b // tq
    kv_spec = pl.BlockSpec((None, None, mem_len, HEAD_DIM), lambda i, h: (layer, i // nq, 0, h))
    return pl.pallas_call(
        _mem_attn_kernel,
        out_shape=jax.ShapeDtypeStruct((m, MEM_WIDTH), bf16),
        grid=(m // tq, MEM_HEADS),
        in_specs=[pl.BlockSpec((tq, HEAD_DIM), lambda i, h: (i, h)), kv_spec, kv_spec],
        out_specs=pl.BlockSpec((tq, HEAD_DIM), lambda i, h: (i, h)),
        compiler_params=_params(("arbitrary", "arbitrary")),
        name="mem_attn",
    )(q, mem_k, mem_v)


def _ssd_kernel(z_ref, u_ref, fdt_ref, cw_ref, cb_ref, dtb_ref, alog_ref, dexp_ref, gn_ref, e_ref, tri_ref,
                h0_ref, c0_ref, y_ref, hout_ref, cout_ref, ht_ref, cbuf_ref, *, ll, lv):
    c = pl.program_id(1)
    npad = SSD_CONV - 1
    base = SUBLANES
    hpg = SSD_HEADS // SSD_GROUPS
    gw = hpg * SSD_HEAD_DIM

    @pl.when(c == 0)
    def _():
        ht_ref[...] = h0_ref[...].T
        cbuf_ref[base - npad:base, :] = c0_ref[...]
        if lv < ll:
            cbuf_ref[base:base + ll, :] = jnp.zeros((ll, SSD_CONV_CH), f32)

    cbuf_ref[base:base + lv, :] = u_ref[...]
    xbc = cb_ref[...] + sum(cw_ref[i:i + 1, :] * cbuf_ref[base - npad + i:base - npad + i + ll, :]
                            for i in range(SSD_CONV))
    conv_tail = cbuf_ref[base + lv - npad:base + lv, :]
    cbuf_ref[base - npad:base, :] = conv_tail
    xbc = _silu(xbc)

    fdt = fdt_ref[...] if lv == ll else jnp.concatenate([fdt_ref[...], jnp.zeros((ll - lv, LANES), f32)], axis=0)
    dt = jax.nn.softplus(fdt + dtb_ref[...])
    lane = lax.broadcasted_iota(jnp.int32, (ll, LANES), 1)
    row = lax.broadcasted_iota(jnp.int32, (ll, LANES), 0)
    live = (lane >= FDT_LANE0) & (lane < FDT_LANE0 + SSD_HEADS) & (row < lv)
    dt = jnp.where(live, dt, 0.0)
    xs = xbc[:, :SSD_INNER]
    if lv < ll:
        xs = jnp.where(lax.broadcasted_iota(jnp.int32, (ll, SSD_INNER), 0) < lv, xs, 0.0)
    bm = xbc[:, SSD_INNER:SSD_INNER + SSD_GROUPS * SSD_STATE].astype(bf16)
    cm = xbc[:, SSD_INNER + SSD_GROUPS * SSD_STATE:].astype(bf16)

    dta = dt * (-jnp.exp(alog_ref[...]))
    cum = jnp.dot(tri_ref[...], dta, preferred_element_type=f32, precision=HIGHEST)
    cum_t = cum.T
    dt_t = dt.T
    cum_last = cum[ll - 1:ll, :]
    e = e_ref[...]
    ecx = _split_dot(jnp.exp(cum), e)
    tex = _split_dot(jnp.exp(cum_last - cum) * dt, e)
    causal = lax.broadcasted_iota(jnp.int32, (ll, ll), 0) >= lax.broadcasted_iota(jnp.int32, (ll, ll), 1)

    ys = []
    for g in range(SSD_GROUPS):
        cg = cm[:, g * SSD_STATE:(g + 1) * SSD_STATE]
        bg = bm[:, g * SSD_STATE:(g + 1) * SSD_STATE]
        cbm = _dot_nt(cg, bg)
        xg = xs[:, g * gw:(g + 1) * gw]
        htg = ht_ref[:, g * gw:(g + 1) * gw]
        y_parts = []
        for j in range(hpg):
            ln = FDT_LANE0 + g * hpg + j
            seg = cum[:, ln:ln + 1] - cum_t[ln:ln + 1, :]
            decay = jnp.where(causal, jnp.exp(jnp.where(causal, seg, 0.0)), 0.0)
            mh = (cbm * decay * dt_t[ln:ln + 1, :]).astype(bf16)
            y_parts.append(_dot(mh, xg[:, j * SSD_HEAD_DIM:(j + 1) * SSD_HEAD_DIM].astype(bf16)))
        yg = jnp.concatenate(y_parts, axis=1) + _dot(cg, htg.astype(bf16)) * ecx[:, g * gw:(g + 1) * gw]
        ys.append(yg)
        xp = (xg * tex[:, g * gw:(g + 1) * gw]).astype(bf16)
        ht_ref[:, g * gw:(g + 1) * gw] = ecx[ll - 1:ll, g * gw:(g + 1) * gw] * htg + _dot_tn(bg, xp)

    y = jnp.concatenate(ys, axis=1) + dexp_ref[...] * xs
    zz = z_ref[...] if lv == ll else jnp.concatenate([z_ref[...], jnp.zeros((ll - lv, SSD_INNER), f32)], axis=0)
    y = y * _silu(zz)
    outs = []
    for g in range(SSD_GROUPS):
        yg = y[:, g * gw:(g + 1) * gw]
        ms = jnp.mean(yg * yg, axis=-1, keepdims=True)
        outs.append(yg * lax.rsqrt(ms + NORM_EPS))
    yn = jnp.concatenate(outs, axis=1) * gn_ref[...]
    y_ref[...] = yn[:lv].astype(y_ref.dtype)

    @pl.when(c == pl.num_programs(1) - 1)
    def _():
        hout_ref[...] = ht_ref[...].T
        cout_ref[...] = conv_tail


def ssd_mix(z, u, fdt, conv_w, conv_b, dt_bias, a_log, d_skip, gnorm, h0, c0, *, lv):
    m = z.shape[0]
    nseq = h0.shape[0]
    ll = SSD_CHUNK
    assert lv <= ll and lv % SUBLANES == 0 and lv >= SSD_CONV - 1 and m % (nseq * lv) == 0
    nchunk = m // (nseq * lv)
    pad = lambda v: jnp.zeros((1, LANES), f32).at[0, FDT_LANE0:FDT_LANE0 + SSD_HEADS].set(v)
    e = np.zeros((LANES, SSD_INNER), np.float32)
    for h in range(SSD_HEADS):
        e[FDT_LANE0 + h, h * SSD_HEAD_DIM:(h + 1) * SSD_HEAD_DIM] = 1.0
    r = np.arange(ll)
    tri = jnp.asarray((r[:, None] >= r[None, :]).astype(np.float32))
    row = lambda i, c: (i * nchunk + c, 0)
    const = lambda i, c: (0, 0)
    seq3 = lambda i, c: (i, 0, 0)
    return pl.pallas_call(
        functools.partial(_ssd_kernel, ll=ll, lv=lv),
        out_shape=(jax.ShapeDtypeStruct((m, SSD_INNER), bf16),
                   jax.ShapeDtypeStruct((nseq, SSD_INNER, SSD_STATE), f32),
                   jax.ShapeDtypeStruct((nseq, SSD_CONV - 1, SSD_CONV_CH), f32)),
        grid=(nseq, nchunk),
        in_specs=[pl.BlockSpec((lv, SSD_INNER), row), pl.BlockSpec((lv, SSD_CONV_CH), row),
                  pl.BlockSpec((lv, LANES), row),
                  pl.BlockSpec((SSD_CONV, SSD_CONV_CH), const), pl.BlockSpec((1, SSD_CONV_CH), const),
                  pl.BlockSpec((1, LANES), const), pl.BlockSpec((1, LANES), const),
                  pl.BlockSpec((1, SSD_INNER), const), pl.BlockSpec((1, SSD_INNER), const),
                  pl.BlockSpec((LANES, SSD_INNER), const), pl.BlockSpec((ll, ll), const),
                  pl.BlockSpec((None, SSD_INNER, SSD_STATE), seq3),
                  pl.BlockSpec((None, SSD_CONV - 1, SSD_CONV_CH), seq3)],
        out_specs=(pl.BlockSpec((lv, SSD_INNER), row),
                   pl.BlockSpec((None, SSD_INNER, SSD_STATE), seq3),
                   pl.BlockSpec((None, SSD_CONV - 1, SSD_CONV_CH), seq3)),
        scratch_shapes=[pltpu.VMEM((SSD_STATE, SSD_INNER), f32), pltpu.VMEM((SUBLANES + ll, SSD_CONV_CH), f32)],
        compiler_params=_params(("arbitrary", "arbitrary")),
        name="ssd_mix",
    )(z, u, fdt, conv_w, conv_b.reshape(1, -1), pad(dt_bias), pad(a_log),
      jnp.repeat(d_skip, SSD_HEAD_DIM).reshape(1, -1), gnorm.reshape(1, -1), jnp.asarray(e, dtype=bf16), tri, h0, c0)


def _hgrn_kernel(q_ref, f_ref, i_ref, g_ref, la_ref, lb_ref, om_ref, gn_ref, tri_ref, s0_ref,
                 o_ref, sout_ref, st_ref, *, ll, lv):
    c = pl.program_id(1)

    @pl.when(c == 0)
    def _():
        for h in range(HEADS):
            st_ref[h] = s0_ref[h].T

    def padded(ref):
        x = ref[...]
        return x if lv == ll else jnp.concatenate([x, jnp.zeros((ll - lv, WIDTH), f32)], axis=0)

    fl = padded(f_ref)
    a = la_ref[...]
    b = lb_ref[...] + _log_sigmoid(fl)
    lf = jnp.maximum(a, b) + jnp.log(1.0 + jnp.exp(-jnp.abs(a - b)))
    key = om_ref[...] * jax.nn.sigmoid(-fl)
    q = _silu(padded(q_ref))
    v = padded(i_ref)
    if lv < ll:
        live = lax.broadcasted_iota(jnp.int32, (ll, WIDTH), 0) < lv
        lf = jnp.where(live, lf, 0.0)
        key = jnp.where(live, key, 0.0)
    cum = jnp.dot(tri_ref[...], lf, preferred_element_type=f32, precision=HIGHEST)
    gate = _silu(padded(g_ref))

    nsub = ll // HG_SUB
    lrow = lax.broadcasted_iota(jnp.int32, (HG_SUB, HEAD_DIM), 0)
    lane = lax.broadcasted_iota(jnp.int32, (HG_SUB, HG_SUB), 1)
    lrow2 = lax.broadcasted_iota(jnp.int32, (HG_SUB, HG_SUB), 0)
    for h in range(HEADS):
        sl = slice(h * HEAD_DIM, (h + 1) * HEAD_DIM)
        qh, kh, vh, ch = q[:, sl], key[:, sl], v[:, sl], cum[:, sl]
        vb = vh.astype(bf16)
        st = st_ref[h]
        c_last = ch[ll - 1:ll]
        o_rows = []
        for blk in range(nsub):
            r0 = blk * HG_SUB
            cb_, qb_, kb_ = ch[r0:r0 + HG_SUB], qh[r0:r0 + HG_SUB], kh[r0:r0 + HG_SUB]
            diag = jnp.zeros((HG_SUB, HG_SUB), f32)
            for s in range(HG_SUB):
                ok = lrow >= s
                e = jnp.exp(jnp.where(ok, cb_ - cb_[s:s + 1], 0.0))
                val = jnp.sum(jnp.where(ok, qb_ * kb_[s:s + 1] * e, 0.0), axis=-1, keepdims=True)
                diag = jnp.where(lane == s, val, diag)
            diag = jnp.where(lane <= lrow2, diag, 0.0)
            parts = []
            if r0 > 0:
                ref_c = ch[r0 - 1:r0]
                qt = (qb_ * jnp.exp(cb_ - ref_c)).astype(bf16)
                kt = (kh[:r0] * jnp.exp(ref_c - ch[:r0])).astype(bf16)
                parts.append(_dot_nt(qt, kt))
            parts.append(diag)
            if r0 + HG_SUB < ll:
                parts.append(jnp.zeros((HG_SUB, ll - r0 - HG_SUB), f32))
            att = jnp.concatenate(parts, axis=1) if len(parts) > 1 else parts[0]
            o_rows.append(_dot(att.astype(bf16), vb))
        oh = jnp.concatenate(o_rows, axis=0) + _dot_nt((qh * jnp.exp(ch)).astype(bf16), st.astype(bf16))
        st_ref[h] = st * jnp.exp(c_last) + _dot_tn(vb, (kh * jnp.exp(c_last - ch)).astype(bf16))
        ms = jnp.mean(oh * oh, axis=-1, keepdims=True)
        on = oh * lax.rsqrt(ms + NORM_EPS) * gn_ref[:, sl] * gate[:, sl]
        o_ref[:, sl] = on[:lv].astype(o_ref.dtype)

    @pl.when(c == pl.num_programs(1) - 1)
    def _():
        for h in range(HEADS):
            sout_ref[h] = st_ref[h].T


def hgrn_mix(proj, lower_bound, gnorm, s0, *, lv):
    m = proj.shape[0]
    nseq = s0.shape[0]
    ll = HG_CHUNK
    assert lv <= ll and lv % SUBLANES == 0 and m % (nseq * lv) == 0
    nchunk = m // (nseq * lv)
    lb = lower_bound.reshape(1, WIDTH).astype(f32)
    r = np.arange(ll)
    tri = jnp.asarray((r[:, None] >= r[None, :]).astype(np.float32))
    col = lambda k: pl.BlockSpec((lv, WIDTH), lambda i, c, k=k: (i * nchunk + c, k))
    const = lambda i, c: (0, 0)
    seq4 = lambda i, c: (i, 0, 0, 0)
    return pl.pallas_call(
        functools.partial(_hgrn_kernel, ll=ll, lv=lv),
        out_shape=(jax.ShapeDtypeStruct((m, WIDTH), bf16),
                   jax.ShapeDtypeStruct((nseq, HEADS, HEAD_DIM, HEAD_DIM), f32)),
        grid=(nseq, nchunk),
        in_specs=[col(0), col(1), col(2), col(3),
                  pl.BlockSpec((1, WIDTH), const), pl.BlockSpec((1, WIDTH), const), pl.BlockSpec((1, WIDTH), const),
                  pl.BlockSpec((1, WIDTH), const), pl.BlockSpec((ll, ll), const),
                  pl.BlockSpec((None, HEADS, HEAD_DIM, HEAD_DIM), seq4)],
        out_specs=(pl.BlockSpec((lv, WIDTH), lambda i, c: (i * nchunk + c, 0)),
                   pl.BlockSpec((None, HEADS, HEAD_DIM, HEAD_DIM), seq4)),
        scratch_shapes=[pltpu.VMEM((HEADS, HEAD_DIM, HEAD_DIM), f32)],
        compiler_params=_params(("arbitrary", "arbitrary")),
        name="hgrn_mix",
    )(proj, proj, proj, proj, jnp.log(lb), jnp.log1p(-lb), 1.0 - lb, gnorm.reshape(1, WIDTH), tri, s0)


def _heads_to_rows(x, nb, n_new):
    return x.reshape(nb, n_new, HEADS, HEAD_DIM).transpose(0, 2, 1, 3).reshape(nb, HEADS * n_new, HEAD_DIM)


def _rows_to_heads(x, nb, n_new):
    return x.reshape(nb, HEADS, n_new, HEAD_DIM).transpose(0, 2, 1, 3).reshape(nb * n_new, WIDTH)


def _pad_keys(x, nb, n_new):
    x = x.reshape(nb, n_new, HEADS, HEAD_DIM).transpose(0, 2, 1, 3)
    return jnp.pad(x, ((0, 0), (0, 0), (0, PAGE - n_new), (0, 0)))


def _trunk(x, mem_k, mem_v, p, lower_bounds, *, nseq, decode):
    m = x.shape[0]
    t = m // nseq
    assert decode or nseq == 1
    tm = min(m, 1024)
    depth = p["norm_ffn1"].shape[0]
    st = {}
    for layer in range(depth):
        i = layer // 2
        hmid = norm_matmul(x, p["norm_ffn1"][layer], [(p["ffn1_gate"], layer, 0), (p["ffn1_up"], layer, 0)],
                           n=D_FF, tn=512, tm=tm, out_dtype=bf16, swiglu=True, name="ffn1_up")
        x = matmul_residual(hmid, p["ffn1_down"], layer, x, scale=0.5, tm=tm, tn=1024, tk=1408, name="ffn1_down")
        gain = p["norm_mix"][layer]
        if layer % 2 == 0:
            w_in = p["ab_w_in"][i]
            qkv = norm_matmul(x, gain, [(w_in, None, 0)], n=3 * WIDTH, tn=512, tm=tm, name="ab_qkv")
            off_z = 3 * WIDTH + HEADS
            off_x = off_z + SSD_INNER
            off_dt = off_x + SSD_CONV_CH
            w_fdt = jnp.concatenate([w_in[:, 3 * WIDTH:off_z], w_in[:, off_dt:off_dt + SSD_HEADS],
                                     jnp.zeros((D_MODEL, LANES - HEADS - SSD_HEADS), f32)], axis=1)
            zz = norm_matmul(x, gain, [(w_in[:, off_z:off_x], None, 0)], n=SSD_INNER, tn=512, tm=tm, name="ab_z")
            uu = norm_matmul(x, gain, [(w_in[:, off_x:off_dt], None, 0)], n=SSD_CONV_CH, tn=512, tm=tm, name="ab_xbc")
            fdt = norm_matmul(x, gain, [(w_fdt, None, 0)], n=LANES, tn=LANES, tm=tm, name="ab_fdt")
            logf, ck = fox_gate(fdt, p["ab_fox_fbias"][i], seg=t, tb=min(m, 512))
            k_new, v_new = qkv[:, WIDTH:2 * WIDTH], qkv[:, 2 * WIDTH:]
            if decode:
                bias_new = jnp.pad((-ck).reshape(HEADS, nseq, t).transpose(1, 0, 2), ((0, 0), (0, 0), (0, PAGE - t)))
                o = decode_attention("fox", _heads_to_rows(qkv[:, :WIDTH], nseq, t), _pad_keys(k_new, nseq, t),
                                     _pad_keys(v_new, nseq, t), p["cache_fox_k"], p["cache_fox_v"], p["page_table"],
                                     bias_new=bias_new, bias_past=fox_past_bias(p["cache_fox_logf"], p["page_table"]).transpose(0, 1, 3, 2))
                o_fox = _rows_to_heads(o, nseq, t).astype(bf16)
                h0, c0, lv = p["state_ssd"][i].reshape(nseq, SSD_INNER, SSD_STATE), p["state_ssd_conv"][i], t
            else:
                o_fox = fox_flash(qkv, ck, tq=512)
                h0 = jnp.zeros((nseq, SSD_INNER, SSD_STATE), f32)
                c0 = jnp.zeros((nseq, SSD_CONV - 1, SSD_CONV_CH), f32)
                lv = SSD_CHUNK
            y, h_new, c_new = ssd_mix(zz, uu, fdt, p["ab_conv_w"][i], p["ab_conv_b"][i], p["ab_dt_bias"][i],
                                      p["ab_A_log"][i], p["ab_D"][i], p["ab_ssd_norm"][i], h0, c0, lv=lv)
            mix_in = jnp.concatenate([o_fox, y], axis=1)
            x = matmul_residual(mix_in, p["ab_w_out"], i, x, scale=1.0, tm=tm, tn=1024, tk=1024, name="ab_out")
            st.setdefault("fox_k", []).append(k_new.reshape(nseq, t, HEADS, HEAD_DIM))
            st.setdefault("fox_v", []).append(v_new.reshape(nseq, t, HEADS, HEAD_DIM))
            st.setdefault("fox_logf", []).append(logf.reshape(nseq, t, HEADS))
            st.setdefault("ssd_conv", []).append(c_new)
            st.setdefault("ssd_state", []).append(h_new.reshape(nseq, SSD_HEADS, SSD_HEAD_DIM, SSD_STATE))
        else:
            proj = norm_matmul(x, gain, [(p["cd_w_in"], i, 0)], n=7 * WIDTH, tn=512, tm=tm, name="cd_in")
            k_new, v_new = proj[:, 5 * WIDTH:6 * WIDTH], proj[:, 6 * WIDTH:]
            if decode:
                s0, lv = p["state_hgrn"][i], t
                o = decode_attention("sb", _heads_to_rows(proj[:, 4 * WIDTH:5 * WIDTH], nseq, t),
                                     _pad_keys(k_new, nseq, t), _pad_keys(v_new, nseq, t),
                                     p["cache_sb_k"], p["cache_sb_v"], p["page_table"])
                o_sb = _rows_to_heads(o, nseq, t).astype(bf16)
            else:
                s0, lv = jnp.zeros((nseq, HEADS, HEAD_DIM, HEAD_DIM), f32), HG_CHUNK
                o_sb = sb_flash(proj, 4 * HEADS, tq=512, tk=256)
            o_hg, s_new = hgrn_mix(proj, lower_bounds[layer], p["cd_hg_norm"][i], s0, lv=lv)
            mix_in = jnp.concatenate([o_hg, o_sb], axis=1)
            x = matmul_residual(mix_in, p["cd_w_out"], i, x, scale=1.0, tm=tm, tn=1024, tk=1024, name="cd_out")
            st.setdefault("hgrn", []).append(s_new)
            st.setdefault("sb_k", []).append(k_new.reshape(nseq, t, HEADS, HEAD_DIM))
            st.setdefault("sb_v", []).append(v_new.reshape(nseq, t, HEADS, HEAD_DIM))
        qm = norm_matmul(x, p["norm_mem"][layer], [(p["mem_wq"], layer, 0)], n=MEM_WIDTH, tn=MEM_WIDTH, tm=tm,
                         name="mem_q")
        om = mem_attention(qm, mem_k, mem_v, layer, tq=min(t, 1024))
        x = matmul_residual(om, p["mem_wo"], layer, x, scale=1.0, tm=tm, tn=1024, tk=MEM_WIDTH, name="mem_out")
        hmid = norm_matmul(x, p["norm_ffn2"][layer], [(p["ffn2_gate"], layer, 0), (p["ffn2_up"], layer, 0)],
                           n=D_FF, tn=512, tm=tm, out_dtype=bf16, swiglu=True, name="ffn2_up")
        x = matmul_residual(hmid, p["ffn2_down"], layer, x, scale=0.5, tm=tm, tn=1024, tk=1408, name="ffn2_down")
    y = rmsnorm_rows(x, p["norm_final"], tm=min(m, 512))
    return y, {k: jnp.stack(v) for k, v in st.items()}


def kernel(x_prompt, x_sample, cache_fox_k, cache_fox_v, cache_fox_logf, state_ssd_conv, state_ssd, state_hgrn, cache_sb_k, cache_sb_v, cache_mem_k, cache_mem_v, page_table, mem_prompt, norm_ffn1, ffn1_gate, ffn1_up, ffn1_down, norm_mix, ab_w_in, ab_fox_fbias, ab_conv_w, ab_conv_b, ab_dt_bias, ab_A_log, ab_D, ab_ssd_norm, ab_w_out, cd_w_in, hg_lower_bound, cd_hg_norm, cd_w_out, norm_mem, norm_memkv, mem_wq, mem_wk, mem_wv, mem_wo, norm_ffn2, ffn2_gate, ffn2_up, ffn2_down, norm_final):
    p = dict(norm_ffn1=norm_ffn1, ffn1_gate=ffn1_gate, ffn1_up=ffn1_up, ffn1_down=ffn1_down, norm_mix=norm_mix,
             ab_w_in=ab_w_in, ab_fox_fbias=ab_fox_fbias, ab_conv_w=ab_conv_w, ab_conv_b=ab_conv_b,
             ab_dt_bias=ab_dt_bias, ab_A_log=ab_A_log, ab_D=ab_D, ab_ssd_norm=ab_ssd_norm, ab_w_out=ab_w_out,
             cd_w_in=cd_w_in, cd_hg_norm=cd_hg_norm, cd_w_out=cd_w_out, norm_mem=norm_mem, mem_wq=mem_wq,
             mem_wo=mem_wo, norm_ffn2=norm_ffn2, ffn2_gate=ffn2_gate, ffn2_up=ffn2_up, ffn2_down=ffn2_down,
             norm_final=norm_final, cache_fox_k=cache_fox_k, cache_fox_v=cache_fox_v, cache_fox_logf=cache_fox_logf,
             cache_sb_k=cache_sb_k, cache_sb_v=cache_sb_v, page_table=page_table, state_ssd=state_ssd,
             state_ssd_conv=state_ssd_conv, state_hgrn=state_hgrn)
    depth = norm_ffn1.shape[0]
    probs = jax.nn.softmax(hg_lower_bound.astype(f32), axis=0)
    lower_bounds = jnp.cumsum(probs, axis=0) - probs[0]

    bp, seq, _ = x_prompt.shape
    bs, dseq, _ = x_sample.shape
    mem_len = mem_prompt.shape[1]
    mem_rows = mem_prompt.reshape(bp * mem_len, D_MODEL)
    mk = [norm_matmul(mem_rows, norm_memkv[l], [(mem_wk, l, 0)], n=MEM_WIDTH, tn=MEM_WIDTH, tm=mem_len, name="mem_k")
          for l in range(depth)]
    mv = [norm_matmul(mem_rows, norm_memkv[l], [(mem_wv, l, 0)], n=MEM_WIDTH, tn=MEM_WIDTH, tm=mem_len, name="mem_v")
          for l in range(depth)]
    mem_k_p = jnp.stack(mk).reshape(depth, bp, mem_len, MEM_WIDTH)
    mem_v_p = jnp.stack(mv).reshape(depth, bp, mem_len, MEM_WIDTH)

    y_p, sp = _trunk(x_prompt.reshape(bp * seq, D_MODEL), mem_k_p, mem_v_p, p, lower_bounds, nseq=bp, decode=False)
    y_s, ss = _trunk(x_sample.reshape(bs * dseq, D_MODEL), cache_mem_k.reshape(depth, bs, mem_len, MEM_WIDTH),
                     cache_mem_v.reshape(depth, bs, mem_len, MEM_WIDTH), p, lower_bounds, nseq=bs, decode=True)

    m5 = (depth, bp, mem_len, MEM_HEADS, HEAD_DIM)
    return (y_p.reshape(bp, seq, D_MODEL), y_s.reshape(bs, dseq, D_MODEL),
            sp["fox_k"], sp["fox_v"], sp["fox_logf"], sp["ssd_conv"], sp["ssd_state"], sp["hgrn"], sp["sb_k"], sp["sb_v"],
            mem_k_p.reshape(m5), mem_v_p.reshape(m5),
            ss["fox_k"], ss["fox_v"], ss["fox_logf"], ss["ssd_conv"], ss["ssd_state"], ss["hgrn"], ss["sb_k"], ss["sb_v"])
```

```python
import functools
import math

import numpy as np
import jax
import jax.numpy as jnp
from jax import lax
from jax.experimental import pallas as pl
from jax.experimental.pallas import tpu as pltpu

f32 = jnp.float32
bf16 = jnp.bfloat16
HIGHEST = lax.Precision.HIGHEST

D_MODEL = 2048
D_FF = 5632
HEADS = 8
HEAD_DIM = 128
WIDTH = HEADS * HEAD_DIM
SSD_HEADS = 32
SSD_HEAD_DIM = 64
SSD_INNER = SSD_HEADS * SSD_HEAD_DIM
SSD_GROUPS = 4
SSD_STATE = 128
SSD_CONV = 4
SSD_CONV_CH = SSD_INNER + 2 * SSD_GROUPS * SSD_STATE
SSD_CHUNK = 128
HG_CHUNK = 64
HG_SUB = 16
MEM_HEADS = 4
MEM_WIDTH = MEM_HEADS * HEAD_DIM
PAGE = 128
NORM_EPS = 1e-6
NEG_INF = -1e30
ATTN_SCALE = HEAD_DIM ** -0.5
LOG2E = math.log2(math.e)

LANES = 128
SUBLANES = 8
VMEM_LIMIT_BYTES = 56 * 1024 * 1024
FDT_LANE0 = 8

NT_DIMS = (((1,), (1,)), ((), ()))
TN_DIMS = (((0,), (0,)), ((), ()))


def _params(sem):
    return pltpu.CompilerParams(dimension_semantics=sem, vmem_limit_bytes=VMEM_LIMIT_BYTES)


def _dot(a, b):
    return jnp.dot(a, b, preferred_element_type=f32)


def _dot_nt(a, b):
    return lax.dot_general(a, b, NT_DIMS, preferred_element_type=f32)


def _dot_tn(a, b):
    return lax.dot_general(a, b, TN_DIMS, preferred_element_type=f32)


def _split_dot(x, e):
    hi = x.astype(bf16)
    lo = (x - hi.astype(f32)).astype(bf16)
    return _dot(jnp.concatenate([hi, lo], axis=1), jnp.concatenate([e, e], axis=0))


def _silu(x):
    return x * jax.nn.sigmoid(x)


def _log_sigmoid(x):
    return jnp.minimum(x, 0.0) - jnp.log(1.0 + jnp.exp(-jnp.abs(x)))


def _nmm_kernel(x_ref, g_ref, *refs, n_w, swiglu):
    w_refs, o_ref, xn_ref = refs[:n_w], refs[n_w], refs[n_w + 1]

    @pl.when(pl.program_id(1) == 0)
    def _():
        x = x_ref[...]
        ms = jnp.mean(x * x, axis=-1, keepdims=True)
        xn_ref[...] = (x * lax.rsqrt(ms + NORM_EPS) * g_ref[...]).astype(bf16)

    xn = xn_ref[...]
    if swiglu:
        g = _dot(xn, w_refs[0][...].astype(bf16))
        u = _dot(xn, w_refs[1][...].astype(bf16))
        o_ref[...] = (_silu(g) * u).astype(o_ref.dtype)
    else:
        o_ref[...] = _dot(xn, w_refs[0][...].astype(bf16)).astype(o_ref.dtype)


def _w_spec(w, lead, k, tn, off):
    if w.ndim == 3:
        return pl.BlockSpec((None, k, tn), lambda i, j: (lead, 0, j + off))
    return pl.BlockSpec((k, tn), lambda i, j: (0, j + off))


def norm_matmul(x, gain, ws, *, n, tn, tm, out_dtype=f32, swiglu=False, name="nmm"):
    m, k = x.shape
    assert m % tm == 0 and n % tn == 0
    in_specs = [pl.BlockSpec((tm, k), lambda i, j: (i, 0)), pl.BlockSpec((1, k), lambda i, j: (0, 0))]
    in_specs += [_w_spec(w, lead, k, tn, off) for (w, lead, off) in ws]
    return pl.pallas_call(
        functools.partial(_nmm_kernel, n_w=len(ws), swiglu=swiglu),
        out_shape=jax.ShapeDtypeStruct((m, n), out_dtype),
        grid=(m // tm, n // tn),
        in_specs=in_specs,
        out_specs=pl.BlockSpec((tm, tn), lambda i, j: (i, j)),
        scratch_shapes=[pltpu.VMEM((tm, k), bf16)],
        compiler_params=_params(("arbitrary", "arbitrary")),
        name=name,
    )(x, gain.reshape(1, k), *[w for (w, _, _) in ws])


def _mmr_kernel(a_ref, w_ref, r_ref, o_ref, acc_ref, *, nk, scale):
    k = pl.program_id(2)
    p = _dot(a_ref[...].astype(bf16), w_ref[...].astype(bf16))

    @pl.when(k == 0)
    def _():
        acc_ref[...] = p

    @pl.when(k > 0)
    def _():
        acc_ref[...] += p

    @pl.when(k == nk - 1)
    def _():
        o_ref[...] = r_ref[...] + scale * acc_ref[...]


def matmul_residual(a, w, lead, res, *, scale, tm, tn, tk, name="mmr"):
    m, kk = a.shape
    n = res.shape[1]
    assert m % tm == 0 and n % tn == 0 and kk % tk == 0
    nk = kk // tk
    if w.ndim == 3:
        w_spec = pl.BlockSpec((None, tk, tn), lambda i, j, k: (lead, k, j))
    else:
        w_spec = pl.BlockSpec((tk, tn), lambda i, j, k: (k, j))
    return pl.pallas_call(
        functools.partial(_mmr_kernel, nk=nk, scale=scale),
        out_shape=jax.ShapeDtypeStruct((m, n), f32),
        grid=(m // tm, n // tn, nk),
        in_specs=[pl.BlockSpec((tm, tk), lambda i, j, k: (i, k)), w_spec,
                  pl.BlockSpec((tm, tn), lambda i, j, k: (i, j))],
        out_specs=pl.BlockSpec((tm, tn), lambda i, j, k: (i, j)),
        scratch_shapes=[pltpu.VMEM((tm, tn), f32)],
        compiler_params=_params(("arbitrary", "arbitrary", "arbitrary")),
        name=name,
    )(a, w, res)


def _rmsnorm_kernel(x_ref, g_ref, o_ref):
    x = x_ref[...]
    ms = jnp.mean(x * x, axis=-1, keepdims=True)
    o_ref[...] = x * lax.rsqrt(ms + NORM_EPS) * g_ref[...]


def rmsnorm_rows(x, gain, *, tm):
    m, k = x.shape
    return pl.pallas_call(
        _rmsnorm_kernel,
        out_shape=jax.ShapeDtypeStruct((m, k), f32),
        grid=(m // tm,),
        in_specs=[pl.BlockSpec((tm, k), lambda i: (i, 0)), pl.BlockSpec((1, k), lambda i: (0, 0))],
        out_specs=pl.BlockSpec((tm, k), lambda i: (i, 0)),
        compiler_params=_params(("arbitrary",)),
        name="final_norm",
    )(x, gain.reshape(1, k))


def _fox_gate_kernel(fdt_ref, bias_ref, u_ref, lf_ref, ck_ref, carry_ref, *, tb, seg, carry):
    lf = _log_sigmoid(fdt_ref[...] + bias_ref[...])
    lf_ref[...] = lf[:, :HEADS]
    lft = lf.T[:HEADS, :]
    cum = jnp.dot(lft, u_ref[...], preferred_element_type=f32, precision=HIGHEST)
    if carry:
        @pl.when((pl.program_id(0) * tb) % seg == 0)
        def _():
            carry_ref[...] = jnp.zeros_like(carry_ref)
        cum = cum + carry_ref[:, 0:1]
        carry_ref[...] = jnp.broadcast_to(cum[:, tb - 1:tb], carry_ref.shape)
    ck_ref[...] = cum


def fox_gate(fdt, fbias, *, seg, tb):
    m = fdt.shape[0]
    assert m % tb == 0 and (seg % tb == 0 or tb % seg == 0)
    r = np.arange(tb)
    u = ((r[:, None] <= r[None, :]) & (r[:, None] // seg == r[None, :] // seg)).astype(np.float32)
    bias = jnp.zeros((1, LANES), f32).at[0, :HEADS].set(fbias)
    return pl.pallas_call(
        functools.partial(_fox_gate_kernel, tb=tb, seg=seg, carry=seg > tb),
        out_shape=(jax.ShapeDtypeStruct((m, HEADS), f32), jax.ShapeDtypeStruct((HEADS, m), f32)),
        grid=(m // tb,),
        in_specs=[pl.BlockSpec((tb, LANES), lambda i: (i, 0)), pl.BlockSpec((1, LANES), lambda i: (0, 0)),
                  pl.BlockSpec((tb, tb), lambda i: (0, 0))],
        out_specs=(pl.BlockSpec((tb, HEADS), lambda i: (i, 0)), pl.BlockSpec((HEADS, tb), lambda i: (0, i))),
        scratch_shapes=[pltpu.VMEM((HEADS, LANES), f32)],
        compiler_params=_params(("arbitrary",)),
        name="fox_gate",
    )(fdt, bias, jnp.asarray(u))


def _lanes(x, width):
    if width <= LANES:
        return x[:, :width]
    return jnp.concatenate([x] * (width // LANES), axis=1)


def _softmax_step(s, vb, state):
    m_prev, l_prev, acc = state
    m_new = jnp.maximum(m_prev, jnp.max(s, axis=-1, keepdims=True))
    alpha = jnp.exp2(m_prev - m_new)
    p = jnp.exp2(s - _lanes(m_new, s.shape[1]))
    l_new = alpha * l_prev + jnp.sum(p, axis=-1, keepdims=True)
    return m_new, l_new, alpha * acc + _dot(p.astype(bf16), vb)


def _softmax_update(s, vb, m_ref, l_ref, acc_ref, rows):
    m_ref[rows, :], l_ref[rows, :], acc_ref[rows, :] = _softmax_step(
        s, vb, (m_ref[rows, :], l_ref[rows, :], acc_ref[rows, :]))


def _fox_flash_kernel(q_ref, k_ref, v_ref, ck_ref, o_ref, qs_ref, m_ref, l_ref, acc_ref, *, tq, tk, sub):
    i = pl.program_id(1)
    nkb = tq // tk
    qs_ref[...] = (q_ref[...] * (ATTN_SCALE * LOG2E)).astype(bf16)
    m_ref[...] = jnp.full(m_ref.shape, NEG_INF, f32)
    l_ref[...] = jnp.zeros(l_ref.shape, f32)
    acc_ref[...] = jnp.zeros(acc_ref.shape, f32)

    def step(j, diag):
        start = pl.multiple_of(j * tk, tk)
        kb = k_ref[pl.ds(start, tk), :].astype(bf16)
        vb = v_ref[pl.ds(start, tk), :].astype(bf16)
        bias = ck_ref[:, pl.ds(start, tk)]
        for r0 in range(0, tq, sub):
            masked = False
            if diag is not None:
                if diag * tk > r0 + sub - 1:
                    continue
                masked = diag * tk + tk - 1 > r0
            rows = slice(r0, r0 + sub)
            s = _dot_nt(qs_ref[rows, :], kb) - bias
            if masked:
                row = lax.broadcasted_iota(jnp.int32, (sub, tk), 0) + r0
                col = lax.broadcasted_iota(jnp.int32, (sub, tk), 1) + diag * tk
                s = jnp.where(col <= row, s, NEG_INF)
            _softmax_update(s, vb, m_ref, l_ref, acc_ref, rows)

    def body(j, c):
        step(j, None)
        return c

    lax.fori_loop(0, i * nkb, body, 0)
    for d in range(nkb):
        step(i * nkb + d, d)
    o_ref[...] = (acc_ref[...] / l_ref[...]).astype(o_ref.dtype)


def fox_flash(qkv, ck, *, tq, tk, sub):
    t = qkv.shape[0]
    assert t % tq == 0 and tq % tk == 0 and tq % sub == 0 and tk % LANES == 0
    stat = pltpu.VMEM((tq, LANES), f32)
    return pl.pallas_call(
        functools.partial(_fox_flash_kernel, tq=tq, tk=tk, sub=sub),
        out_shape=jax.ShapeDtypeStruct((t, WIDTH), bf16),
        grid=(HEADS, t // tq),
        in_specs=[pl.BlockSpec((tq, HEAD_DIM), lambda h, i: (i, h)),
                  pl.BlockSpec((t, HEAD_DIM), lambda h, i: (0, HEADS + h)),
                  pl.BlockSpec((t, HEAD_DIM), lambda h, i: (0, 2 * HEADS + h)),
                  pl.BlockSpec((None, 1, t), lambda h, i: (h, 0, 0))],
        out_specs=pl.BlockSpec((tq, HEAD_DIM), lambda h, i: (i, h)),
        scratch_shapes=[pltpu.VMEM((tq, HEAD_DIM), bf16), stat, stat, pltpu.VMEM((tq, HEAD_DIM), f32)],
        compiler_params=_params(("arbitrary", "arbitrary")),
        name="fox_flash",
    )(qkv, qkv, qkv, (ck * LOG2E).reshape(HEADS, 1, t))


def _sb_tile(z, valid, tri_ext, r):
    nr, sk = z.shape
    cw = tri_ext.shape[0]
    ls = jnp.minimum(z, 0.0) - jnp.log2(1.0 + jnp.exp2(-jnp.abs(z)))
    lk = ls - z
    if valid is not None:
        lk = jnp.where(valid, lk, 0.0)
    nch = sk // cw
    stacked = lk if nch == 1 else jnp.concatenate([lk[:, c * cw:(c + 1) * cw] for c in range(nch)], axis=0)
    inc = _split_dot(stacked, tri_ext)
    sums = [None] * nch
    for c in range(nch - 1, -1, -1):
        blk = inc[c * nr:(c + 1) * nr]
        sums[c] = blk[:, :cw] + _lanes(r, cw)
        r = r + blk[:, cw:]
    after = sums[0] if nch == 1 else jnp.concatenate(sums, axis=1)
    w = jnp.exp2(ls + (after - lk))
    if valid is not None:
        w = jnp.where(valid, w, 0.0)
    return w, r


def _sb_flash_kernel(q_ref, k_ref, v_ref, tri_ref, o_ref, qs_ref, r_ref, acc_ref, *, tq, tk, sub):
    i = pl.program_id(1)
    nkb = tq // tk
    qs_ref[...] = (q_ref[...] * (ATTN_SCALE * LOG2E)).astype(bf16)
    r_ref[...] = jnp.zeros(r_ref.shape, f32)
    acc_ref[...] = jnp.zeros(acc_ref.shape, f32)

    def step(j, diag):
        start = pl.multiple_of(j * tk, tk)
        kb = k_ref[pl.ds(start, tk), :].astype(bf16)
        vb = v_ref[pl.ds(start, tk), :].astype(bf16)
        for r0 in range(0, tq, sub):
            valid = None
            if diag is not None:
                if diag * tk >= r0 + sub - 1:
                    continue
                if diag * tk + tk - 1 >= r0:
                    row = lax.broadcasted_iota(jnp.int32, (sub, tk), 0) + r0
                    col = lax.broadcasted_iota(jnp.int32, (sub, tk), 1) + diag * tk
                    valid = col < row
            rows = slice(r0, r0 + sub)
            w, r_new = _sb_tile(_dot_nt(qs_ref[rows, :], kb), valid, tri_ref[...], r_ref[rows, :])
            acc_ref[rows, :] += _dot(w.astype(bf16), vb)
            r_ref[rows, :] = r_new

    for d in range(nkb - 1, -1, -1):
        step(i * nkb + d, d)

    def body(t, c):
        step(i * nkb - 1 - t, None)
        return c

    lax.fori_loop(0, i * nkb, body, 0)
    o_ref[...] = acc_ref[...].astype(o_ref.dtype)


def _tri_ext(n):
    r = np.arange(n)
    tri = (r[:, None] >= r[None, :]).astype(np.float32)
    return jnp.asarray(np.concatenate([tri, np.ones((n, LANES), np.float32)], axis=1), dtype=bf16)


def sb_flash(proj, col0, *, tq, tk, sub):
    t = proj.shape[0]
    assert t % tq == 0 and tq % tk == 0 and tq % sub == 0 and tk % LANES == 0
    return pl.pallas_call(
        functools.partial(_sb_flash_kernel, tq=tq, tk=tk, sub=sub),
        out_shape=jax.ShapeDtypeStruct((t, WIDTH), bf16),
        grid=(HEADS, t // tq),
        in_specs=[pl.BlockSpec((tq, HEAD_DIM), lambda h, i: (i, col0 + h)),
                  pl.BlockSpec((t, HEAD_DIM), lambda h, i: (0, col0 + HEADS + h)),
                  pl.BlockSpec((t, HEAD_DIM), lambda h, i: (0, col0 + 2 * HEADS + h)),
                  pl.BlockSpec((LANES, 2 * LANES), lambda h, i: (0, 0))],
        out_specs=pl.BlockSpec((tq, HEAD_DIM), lambda h, i: (i, h)),
        scratch_shapes=[pltpu.VMEM((tq, HEAD_DIM), bf16), pltpu.VMEM((tq, LANES), f32),
                        pltpu.VMEM((tq, HEAD_DIM), f32)],
        compiler_params=_params(("arbitrary", "arbitrary")),
        name="sb_flash",
    )(proj, proj, proj, _tri_ext(LANES))


PAGE_TILE_ROWS = PAGE * HEADS // LANES
TOK_PER_ROW = LANES // HEADS


def _past_bias_kernel(pt_ref, lf_hbm, dm_ref, hm_ref, up_ref, o_ref, buf, sem, *, n_pages):
    b = pl.program_id(0)

    def copy(j):
        return pltpu.make_async_copy(lf_hbm.at[pt_ref[b, j]], buf.at[j], sem)

    def start(j, c):
        copy(j).start()
        return c

    def wait(j, c):
        copy(j).wait()
        return c

    lax.fori_loop(0, n_pages, start, 0)
    lax.fori_loop(0, n_pages, wait, 0)

    def hdot(a, m_ref):
        return jnp.dot(a, m_ref[...], preferred_element_type=f32, precision=HIGHEST)

    ys = [buf[:, r, :] for r in range(PAGE_TILE_ROWS)]
    later = [None] * PAGE_TILE_ROWS
    acc = jnp.zeros_like(ys[0])
    for r in range(PAGE_TILE_ROWS - 1, -1, -1):
        later[r] = acc
        acc = acc + ys[r]
    pages_after = hdot(jnp.dot(up_ref[...], acc, preferred_element_type=f32, precision=HIGHEST), hm_ref)
    for r in range(PAGE_TILE_ROWS):
        o_ref[:, r, :] = hdot(ys[r], dm_ref) + hdot(later[r], hm_ref) + pages_after


def fox_past_bias(cache_logf, page_table):
    nb, n_pages = page_table.shape
    n_pool = cache_logf.shape[1]
    lane = np.arange(LANES)
    same_head = lane[:, None] % HEADS == lane[None, :] % HEADS
    dm = (same_head & (lane[:, None] // HEADS > lane[None, :] // HEADS)).astype(np.float32)
    hm = same_head.astype(np.float32)
    pg = np.arange(n_pages)
    up = (pg[None, :] > pg[:, None]).astype(np.float32)
    const = lambda shape: pl.BlockSpec(shape, lambda b, pt: (0, 0))
    return pl.pallas_call(
        functools.partial(_past_bias_kernel, n_pages=n_pages),
        out_shape=jax.ShapeDtypeStruct((nb, n_pages, PAGE_TILE_ROWS, LANES), f32),
        grid_spec=pltpu.PrefetchScalarGridSpec(
            num_scalar_prefetch=1,
            grid=(nb,),
            in_specs=[pl.BlockSpec(memory_space=pl.ANY), const((LANES, LANES)), const((LANES, LANES)),
                      const((n_pages, n_pages))],
            out_specs=pl.BlockSpec((None, n_pages, PAGE_TILE_ROWS, LANES), lambda b, pt: (b, 0, 0, 0)),
            scratch_shapes=[pltpu.VMEM((n_pages, PAGE_TILE_ROWS, LANES), f32), pltpu.SemaphoreType.DMA(())],
        ),
        compiler_params=_params(("arbitrary",)),
        name="fox_past_bias",
    )(page_table, cache_logf.reshape(n_pool, PAGE_TILE_ROWS, LANES), jnp.asarray(dm), jnp.asarray(hm), jnp.asarray(up))


def _decode_kernel(pt_ref, q_ref, kn_ref, vn_ref, *refs, mode, pp, n_new):
    k_refs, v_refs = refs[:pp], refs[pp:2 * pp]
    refs = refs[2 * pp:]
    if mode == "fox":
        bn_ref, ap_refs = refs[0], refs[1:1 + pp]
        o_ref = refs[1 + pp]
        state_refs = refs[2 + pp:]
    else:
        tri_ref = refs[0]
        o_ref = refs[1]
        state_refs = refs[2:]
    j = pl.program_id(1)
    rows = HEADS * n_new
    page_cols = PAGE * HEADS
    qs = (q_ref[...] * (ATTN_SCALE * LOG2E)).astype(bf16)

    def own_head(cols):
        rh = lax.broadcasted_iota(jnp.int32, (rows, cols), 0) // n_new
        ch = lax.broadcasted_iota(jnp.int32, (rows, cols), 1) % HEADS
        return rh == ch

    def fox_block(kbs, vbs, biases, valid, state):
        m_prev, l_prev, acc = state
        ss = [jnp.where(valid, _dot_nt(qs, kb) + bias, NEG_INF) for kb, bias in zip(kbs, biases)]
        m_new = m_prev
        for s in ss:
            m_new = jnp.maximum(m_new, jnp.max(s, axis=-1, keepdims=True))
        alpha = jnp.exp2(m_prev - m_new)
        l_new, acc = alpha * l_prev, alpha * acc
        for s, vb in zip(ss, vbs):
            p = jnp.exp2(s - _lanes(m_new, s.shape[1]))
            l_new = l_new + jnp.sum(p, axis=-1, keepdims=True)
            acc = acc + _dot(p.astype(bf16), vb)
        return m_new, l_new, acc

    def sb_block(kbs, vbs, valid, state):
        r, acc = state
        cols = kbs[0].shape[0]
        cw = min(cols, LANES)
        tri = tri_ref[...] if cw == LANES else jnp.concatenate([tri_ref[:cw, :cw], tri_ref[:cw, LANES:]], axis=1)
        n = len(kbs)
        z = jnp.concatenate([_dot_nt(qs, kb) for kb in kbs], axis=1) if n > 1 else _dot_nt(qs, kbs[0])
        w, r = _sb_tile(z, jnp.concatenate([valid] * n, axis=1) if n > 1 else valid, tri, r)
        for p, vb in enumerate(vbs):
            acc = acc + _dot(w[:, p * cols:(p + 1) * cols].astype(bf16), vb)
        return r, acc

    @pl.when(j == 0)
    def _():
        cols = n_new * HEADS
        tq = lax.broadcasted_iota(jnp.int32, (rows, cols), 0) % n_new
        sk = lax.broadcasted_iota(jnp.int32, (rows, cols), 1) // HEADS
        kb, vb = kn_ref[...].astype(bf16), vn_ref[...].astype(bf16)
        zeros = jnp.zeros((rows, LANES), f32)
        if mode == "fox":
            state = fox_block([kb], [vb], [bn_ref[...] * LOG2E], own_head(cols) & (sk <= tq),
                              (jnp.full((rows, LANES), NEG_INF, f32), zeros, zeros))
        else:
            state = sb_block([kb], [vb], own_head(cols) & (sk < tq), (zeros, zeros))
        for ref, val in zip(state_refs, state):
            ref[...] = val

    kbs = [k_refs[p][...].reshape(page_cols, HEAD_DIM).astype(bf16) for p in range(pp)]
    vbs = [v_refs[p][...].reshape(page_cols, HEAD_DIM).astype(bf16) for p in range(pp)]
    state = tuple(ref[...] for ref in state_refs)
    if mode == "fox":
        biases = []
        for p in range(pp):
            ap = ap_refs[p][...] * LOG2E
            biases.append(jnp.concatenate(
                [jnp.broadcast_to(ap[r:r + 1], (rows, LANES)) for r in range(PAGE_TILE_ROWS)], axis=1))
        state = fox_block(kbs, vbs, biases, own_head(page_cols), state)
    else:
        state = sb_block(kbs, vbs, own_head(page_cols), state)
    for ref, val in zip(state_refs, state):
        ref[...] = val

    @pl.when(j == pl.num_programs(1) - 1)
    def _():
        if mode == "fox":
            o_ref[...] = state[2] / state[1]
        else:
            o_ref[...] = state[1]


def decode_attention(mode, q, k_new, v_new, cache_k, cache_v, page_table, *, bias_new=None, bias_past=None, pp=8):
    nb, rows, _ = q.shape
    n_new = rows // HEADS
    n_pages = page_table.shape[1]
    assert n_pages % pp == 0
    nsteps = n_pages // pp

    def page_of(b, j, pt, p):
        return pt[b, (nsteps - 1 - j) * pp + p]

    kv_specs = [pl.BlockSpec((None, None, PAGE, HEADS, HEAD_DIM),
                             lambda b, j, pt, p=p: (0, page_of(b, j, pt, p), 0, 0, 0)) for p in range(pp)]
    new_spec = pl.BlockSpec((None, rows, HEAD_DIM), lambda b, j, pt: (b, 0, 0))
    in_specs = [new_spec, new_spec, new_spec] + kv_specs + kv_specs
    args = [q, k_new, v_new] + [cache_k] * pp + [cache_v] * pp
    stat = pltpu.VMEM((rows, LANES), f32)
    if mode == "fox":
        in_specs.append(pl.BlockSpec((None, 1, rows), lambda b, j, pt: (b, 0, 0)))
        in_specs += [pl.BlockSpec((None, None, PAGE_TILE_ROWS, LANES),
                                  lambda b, j, pt, p=p: (b, (nsteps - 1 - j) * pp + p, 0, 0)) for p in range(pp)]
        args += [bias_new] + [bias_past] * pp
        scratch = [stat, stat, pltpu.VMEM((rows, HEAD_DIM), f32)]
    else:
        in_specs.append(pl.BlockSpec((LANES, 2 * LANES), lambda b, j, pt: (0, 0)))
        args.append(_tri_ext(LANES))
        scratch = [stat, pltpu.VMEM((rows, HEAD_DIM), f32)]
    return pl.pallas_call(
        functools.partial(_decode_kernel, mode=mode, pp=pp, n_new=n_new),
        out_shape=jax.ShapeDtypeStruct((nb, rows, HEAD_DIM), f32),
        grid_spec=pltpu.PrefetchScalarGridSpec(
            num_scalar_prefetch=1,
            grid=(nb, nsteps),
            in_specs=in_specs,
            out_specs=pl.BlockSpec((None, rows, HEAD_DIM), lambda b, j, pt: (b, 0, 0)),
            scratch_shapes=scratch,
        ),
        compiler_params=_params(("arbitrary", "arbitrary")),
        name=mode + "_decode",
    )(page_table, *args)


def _mem_attn_kernel(q_ref, k_ref, v_ref, o_ref):
    s = _dot_nt(q_ref[...].astype(bf16), k_ref[...].astype(bf16)) * ATTN_SCALE
    m = jnp.max(s, axis=-1, keepdims=True)
    p = jnp.exp(s - m)
    l = jnp.sum(p, axis=-1, keepdims=True)
    o_ref[...] = (_dot(p.astype(bf16), v_ref[...].astype(bf16)) / l).astype(o_ref.dtype)


def mem_attention(q, mem_k, mem_v, layer, *, tq):
    m = q.shape[0]
    nb, mem_len = mem_k.shape[1], mem_k.shape[2]
    per_b = m // nb
    assert per_b % tq == 0
    nq = per_b // tq
    kv_spec = pl.BlockSpec((None, None, mem_len, HEAD_DIM), lambda i, h: (layer, i // nq, 0, h))
    return pl.pallas_call(
        _mem_attn_kernel,
        out_shape=jax.ShapeDtypeStruct((m, MEM_WIDTH), bf16),
        grid=(m // tq, MEM_HEADS),
        in_specs=[pl.BlockSpec((tq, HEAD_DIM), lambda i, h: (i, h)), kv_spec, kv_spec],
        out_specs=pl.BlockSpec((tq, HEAD_DIM), lambda i, h: (i, h)),
        compiler_params=_params(("arbitrary", "arbitrary")),
        name="mem_attn",
    )(q, mem_k, mem_v)


def _ssd_kernel(z_ref, u_ref, fdt_ref, cw_ref, cb_ref, dtb_ref, alog_ref, dexp_ref, gn_ref, e_ref, tri_ref,
                h0_ref, c0_ref, y_ref, hout_ref, cout_ref, ht_ref, cbuf_ref, *, ll, lv):
    c = pl.program_id(1)
    npad = SSD_CONV - 1
    base = SUBLANES
    hpg = SSD_HEADS // SSD_GROUPS
    gw = hpg * SSD_HEAD_DIM

    @pl.when(c == 0)
    def _():
        ht_ref[...] = h0_ref[...].T
        cbuf_ref[base - npad:base, :] = c0_ref[...]
        if lv < ll:
            cbuf_ref[base:base + ll, :] = jnp.zeros((ll, SSD_CONV_CH), f32)

    cbuf_ref[base:base + lv, :] = u_ref[...]
    xbc = cb_ref[...] + sum(cw_ref[i:i + 1, :] * cbuf_ref[base - npad + i:base - npad + i + ll, :]
                            for i in range(SSD_CONV))
    conv_tail = cbuf_ref[base + lv - npad:base + lv, :]
    cbuf_ref[base - npad:base, :] = conv_tail
    xbc = _silu(xbc)

    fdt = fdt_ref[...] if lv == ll else jnp.concatenate([fdt_ref[...], jnp.zeros((ll - lv, LANES), f32)], axis=0)
    dt = jax.nn.softplus(fdt + dtb_ref[...])
    lane = lax.broadcasted_iota(jnp.int32, (ll, LANES), 1)
    row = lax.broadcasted_iota(jnp.int32, (ll, LANES), 0)
    live = (lane >= FDT_LANE0) & (lane < FDT_LANE0 + SSD_HEADS) & (row < lv)
    dt = jnp.where(live, dt, 0.0)
    xs = xbc[:, :SSD_INNER]
    if lv < ll:
        xs = jnp.where(lax.broadcasted_iota(jnp.int32, (ll, SSD_INNER), 0) < lv, xs, 0.0)
    bm = xbc[:, SSD_INNER:SSD_INNER + SSD_GROUPS * SSD_STATE].astype(bf16)
    cm = xbc[:, SSD_INNER + SSD_GROUPS * SSD_STATE:].astype(bf16)

    dta = dt * (-jnp.exp(alog_ref[...]))
    cum = jnp.dot(tri_ref[...], dta, preferred_element_type=f32, precision=HIGHEST)
    cum_t = cum.T
    dt_t = dt.T
    cum_last = cum[ll - 1:ll, :]
    e = e_ref[...]
    ecx = _split_dot(jnp.exp(cum), e)
    tex = _split_dot(jnp.exp(cum_last - cum) * dt, e)
    causal = lax.broadcasted_iota(jnp.int32, (ll, ll), 0) >= lax.broadcasted_iota(jnp.int32, (ll, ll), 1)

    ys = []
    for g in range(SSD_GROUPS):
        cg = cm[:, g * SSD_STATE:(g + 1) * SSD_STATE]
        bg = bm[:, g * SSD_STATE:(g + 1) * SSD_STATE]
        cbm = _dot_nt(cg, bg)
        xg = xs[:, g * gw:(g + 1) * gw]
        htg = ht_ref[:, g * gw:(g + 1) * gw]
        y_parts = []
        for j in range(hpg):
            ln = FDT_LANE0 + g * hpg + j
            seg = cum[:, ln:ln + 1] - cum_t[ln:ln + 1, :]
            decay = jnp.where(causal, jnp.exp(jnp.where(causal, seg, 0.0)), 0.0)
            mh = (cbm * decay * dt_t[ln:ln + 1, :]).astype(bf16)
            y_parts.append(_dot(mh, xg[:, j * SSD_HEAD_DIM:(j + 1) * SSD_HEAD_DIM].astype(bf16)))
        yg = jnp.concatenate(y_parts, axis=1) + _dot(cg, htg.astype(bf16)) * ecx[:, g * gw:(g + 1) * gw]
        ys.append(yg)
        xp = (xg * tex[:, g * gw:(g + 1) * gw]).astype(bf16)
        ht_ref[:, g * gw:(g + 1) * gw] = ecx[ll - 1:ll, g * gw:(g + 1) * gw] * htg + _dot_tn(bg, xp)

    y = jnp.concatenate(ys, axis=1) + dexp_ref[...] * xs
    zz = z_ref[...] if lv == ll else jnp.concatenate([z_ref[...], jnp.zeros((ll - lv, SSD_INNER), f32)], axis=0)
    y = y * _silu(zz)
    outs = []
    for g in range(SSD_GROUPS):
        yg = y[:, g * gw:(g + 1) * gw]
        ms = jnp.mean(yg * yg, axis=-1, keepdims=True)
        outs.append(yg * lax.rsqrt(ms + NORM_EPS))
    yn = jnp.concatenate(outs, axis=1) * gn_ref[...]
    y_ref[...] = yn[:lv].astype(y_ref.dtype)

    @pl.when(c == pl.num_programs(1) - 1)
    def _():
        hout_ref[...] = ht_ref[...].T
        cout_ref[...] = conv_tail


def ssd_mix(z, u, fdt, conv_w, conv_b, dt_bias, a_log, d_skip, gnorm, h0, c0, *, lv):
    m = z.shape[0]
    nseq = h0.shape[0]
    ll = SSD_CHUNK
    assert lv <= ll and lv % SUBLANES == 0 and lv >= SSD_CONV - 1 and m % (nseq * lv) == 0
    nchunk = m // (nseq * lv)
    pad = lambda v: jnp.zeros((1, LANES), f32).at[0, FDT_LANE0:FDT_LANE0 + SSD_HEADS].set(v)
    e = np.zeros((LANES, SSD_INNER), np.float32)
    for h in range(SSD_HEADS):
        e[FDT_LANE0 + h, h * SSD_HEAD_DIM:(h + 1) * SSD_HEAD_DIM] = 1.0
    r = np.arange(ll)
    tri = jnp.asarray((r[:, None] >= r[None, :]).astype(np.float32))
    row = lambda i, c: (i * nchunk + c, 0)
    const = lambda i, c: (0, 0)
    seq3 = lambda i, c: (i, 0, 0)
    return pl.pallas_call(
        functools.partial(_ssd_kernel, ll=ll, lv=lv),
        out_shape=(jax.ShapeDtypeStruct((m, SSD_INNER), bf16),
                   jax.ShapeDtypeStruct((nseq, SSD_INNER, SSD_STATE), f32),
                   jax.ShapeDtypeStruct((nseq, SSD_CONV - 1, SSD_CONV_CH), f32)),
        grid=(nseq, nchunk),
        in_specs=[pl.BlockSpec((lv, SSD_INNER), row), pl.BlockSpec((lv, SSD_CONV_CH), row),
                  pl.BlockSpec((lv, LANES), row),
                  pl.BlockSpec((SSD_CONV, SSD_CONV_CH), const), pl.BlockSpec((1, SSD_CONV_CH), const),
                  pl.BlockSpec((1, LANES), const), pl.BlockSpec((1, LANES), const),
                  pl.BlockSpec((1, SSD_INNER), const), pl.BlockSpec((1, SSD_INNER), const),
                  pl.BlockSpec((LANES, SSD_INNER), const), pl.BlockSpec((ll, ll), const),
                  pl.BlockSpec((None, SSD_INNER, SSD_STATE), seq3),
                  pl.BlockSpec((None, SSD_CONV - 1, SSD_CONV_CH), seq3)],
        out_specs=(pl.BlockSpec((lv, SSD_INNER), row),
                   pl.BlockSpec((None, SSD_INNER, SSD_STATE), seq3),
                   pl.BlockSpec((None, SSD_CONV - 1, SSD_CONV_CH), seq3)),
        scratch_shapes=[pltpu.VMEM((SSD_STATE, SSD_INNER), f32), pltpu.VMEM((SUBLANES + ll, SSD_CONV_CH), f32)],
        compiler_params=_params(("arbitrary", "arbitrary")),
        name="ssd_mix",
    )(z, u, fdt, conv_w, conv_b.reshape(1, -1), pad(dt_bias), pad(a_log),
      jnp.repeat(d_skip, SSD_HEAD_DIM).reshape(1, -1), gnorm.reshape(1, -1), jnp.asarray(e, dtype=bf16), tri, h0, c0)


def _hgrn_kernel(q_ref, f_ref, i_ref, g_ref, la_ref, lb_ref, om_ref, gn_ref, tri_ref, s0_ref,
                 o_ref, sout_ref, st_ref, *, ll, lv):
    c = pl.program_id(1)

    @pl.when(c == 0)
    def _():
        for h in range(HEADS):
            st_ref[h] = s0_ref[h].T

    def padded(ref):
        x = ref[...]
        return x if lv == ll else jnp.concatenate([x, jnp.zeros((ll - lv, WIDTH), f32)], axis=0)

    fl = padded(f_ref)
    a = la_ref[...]
    b = lb_ref[...] + _log_sigmoid(fl)
    lf = jnp.maximum(a, b) + jnp.log(1.0 + jnp.exp(-jnp.abs(a - b)))
    key = om_ref[...] * jax.nn.sigmoid(-fl)
    q = _silu(padded(q_ref))
    v = padded(i_ref)
    if lv < ll:
        live = lax.broadcasted_iota(jnp.int32, (ll, WIDTH), 0) < lv
        lf = jnp.where(live, lf, 0.0)
        key = jnp.where(live, key, 0.0)
    cum = jnp.dot(tri_ref[...], lf, preferred_element_type=f32, precision=HIGHEST)
    gate = _silu(padded(g_ref))

    nsub = ll // HG_SUB
    lrow = lax.broadcasted_iota(jnp.int32, (HG_SUB, HEAD_DIM), 0)
    lane = lax.broadcasted_iota(jnp.int32, (HG_SUB, HG_SUB), 1)
    lrow2 = lax.broadcasted_iota(jnp.int32, (HG_SUB, HG_SUB), 0)
    for h in range(HEADS):
        sl = slice(h * HEAD_DIM, (h + 1) * HEAD_DIM)
        qh, kh, vh, ch = q[:, sl], key[:, sl], v[:, sl], cum[:, sl]
        vb = vh.astype(bf16)
        st = st_ref[h]
        c_last = ch[ll - 1:ll]
        o_rows = []
        for blk in range(nsub):
            r0 = blk * HG_SUB
            cb_, qb_, kb_ = ch[r0:r0 + HG_SUB], qh[r0:r0 + HG_SUB], kh[r0:r0 + HG_SUB]
            diag = jnp.zeros((HG_SUB, HG_SUB), f32)
            for s in range(HG_SUB):
                ok = lrow >= s
                e = jnp.exp(jnp.where(ok, cb_ - cb_[s:s + 1], 0.0))
                val = jnp.sum(jnp.where(ok, qb_ * kb_[s:s + 1] * e, 0.0), axis=-1, keepdims=True)
                diag = jnp.where(lane == s, val, diag)
            diag = jnp.where(lane <= lrow2, diag, 0.0)
            parts = []
            if r0 > 0:
                ref_c = ch[r0 - 1:r0]
                qt = (qb_ * jnp.exp(cb_ - ref_c)).astype(bf16)
                kt = (kh[:r0] * jnp.exp(ref_c - ch[:r0])).astype(bf16)
                parts.append(_dot_nt(qt, kt))
            parts.append(diag)
            if r0 + HG_SUB < ll:
                parts.append(jnp.zeros((HG_SUB, ll - r0 - HG_SUB), f32))
            att = jnp.concatenate(parts, axis=1) if len(parts) > 1 else parts[0]
            o_rows.append(_dot(att.astype(bf16), vb))
        oh = jnp.concatenate(o_rows, axis=0) + _dot_nt((qh * jnp.exp(ch)).astype(bf16), st.astype(bf16))
        st_ref[h] = st * jnp.exp(c_last) + _dot_tn(vb, (kh * jnp.exp(c_last - ch)).astype(bf16))
        ms = jnp.mean(oh * oh, axis=-1, keepdims=True)
        on = oh * lax.rsqrt(ms + NORM_EPS) * gn_ref[:, sl] * gate[:, sl]
        o_ref[:, sl] = on[:lv].astype(o_ref.dtype)

    @pl.when(c == pl.num_programs(1) - 1)
    def _():
        for h in range(HEADS):
            sout_ref[h] = st_ref[h].T


def hgrn_mix(proj, lower_bound, gnorm, s0, *, lv):
    m = proj.shape[0]
    nseq = s0.shape[0]
    ll = HG_CHUNK
    assert lv <= ll and lv % SUBLANES == 0 and m % (nseq * lv) == 0
    nchunk = m // (nseq * lv)
    lb = lower_bound.reshape(1, WIDTH).astype(f32)
    r = np.arange(ll)
    tri = jnp.asarray((r[:, None] >= r[None, :]).astype(np.float32))
    col = lambda k: pl.BlockSpec((lv, WIDTH), lambda i, c, k=k: (i * nchunk + c, k))
    const = lambda i, c: (0, 0)
    seq4 = lambda i, c: (i, 0, 0, 0)
    return pl.pallas_call(
        functools.partial(_hgrn_kernel, ll=ll, lv=lv),
        out_shape=(jax.ShapeDtypeStruct((m, WIDTH), bf16),
                   jax.ShapeDtypeStruct((nseq, HEADS, HEAD_DIM, HEAD_DIM), f32)),
        grid=(nseq, nchunk),
        in_specs=[col(0), col(1), col(2), col(3),
                  pl.BlockSpec((1, WIDTH), const), pl.BlockSpec((1, WIDTH), const), pl.BlockSpec((1, WIDTH), const),
                  pl.BlockSpec((1, WIDTH), const), pl.BlockSpec((ll, ll), const),
                  pl.BlockSpec((None, HEADS, HEAD_DIM, HEAD_DIM), seq4)],
        out_specs=(pl.BlockSpec((lv, WIDTH), lambda i, c: (i * nchunk + c, 0)),
                   pl.BlockSpec((None, HEADS, HEAD_DIM, HEAD_DIM), seq4)),
        scratch_shapes=[pltpu.VMEM((HEADS, HEAD_DIM, HEAD_DIM), f32)],
        compiler_params=_params(("arbitrary", "arbitrary")),
        name="hgrn_mix",
    )(proj, proj, proj, proj, jnp.log(lb), jnp.log1p(-lb), 1.0 - lb, gnorm.reshape(1, WIDTH), tri, s0)


def _heads_to_rows(x, nb, n_new):
    return x.reshape(nb, n_new, HEADS, HEAD_DIM).transpose(0, 2, 1, 3).reshape(nb, HEADS * n_new, HEAD_DIM)


def _rows_to_heads(x, nb, n_new):
    return x.reshape(nb, HEADS, n_new, HEAD_DIM).transpose(0, 2, 1, 3).reshape(nb * n_new, WIDTH)


def _token_head_rows(x, nb, n_new):
    return x.reshape(nb, n_new * HEADS, HEAD_DIM)


def _trunk(x, mem_k, mem_v, p, lower_bounds, *, nseq, decode):
    m = x.shape[0]
    t = m // nseq
    assert decode or nseq == 1
    tm = min(m, 1024)
    depth = p["norm_ffn1"].shape[0]
    st = {}
    for layer in range(depth):
        i = layer // 2
        hmid = norm_matmul(x, p["norm_ffn1"][layer], [(p["ffn1_gate"], layer, 0), (p["ffn1_up"], layer, 0)],
                           n=D_FF, tn=512, tm=tm, out_dtype=bf16, swiglu=True, name="ffn1_up")
        x = matmul_residual(hmid, p["ffn1_down"], layer, x, scale=0.5, tm=tm, tn=1024, tk=1408, name="ffn1_down")
        gain = p["norm_mix"][layer]
        if layer % 2 == 0:
            w_in = p["ab_w_in"][i]
            qkv = norm_matmul(x, gain, [(w_in, None, 0)], n=3 * WIDTH, tn=512, tm=tm, name="ab_qkv")
            off_z = 3 * WIDTH + HEADS
            off_x = off_z + SSD_INNER
            off_dt = off_x + SSD_CONV_CH
            w_fdt = jnp.concatenate([w_in[:, 3 * WIDTH:off_z], w_in[:, off_dt:off_dt + SSD_HEADS],
                                     jnp.zeros((D_MODEL, LANES - HEADS - SSD_HEADS), f32)], axis=1)
            zz = norm_matmul(x, gain, [(w_in[:, off_z:off_x], None, 0)], n=SSD_INNER, tn=512, tm=tm, name="ab_z")
            uu = norm_matmul(x, gain, [(w_in[:, off_x:off_dt], None, 0)], n=SSD_CONV_CH, tn=512, tm=tm, name="ab_xbc")
            fdt = norm_matmul(x, gain, [(w_fdt, None, 0)], n=LANES, tn=LANES, tm=tm, name="ab_fdt")
            logf, ck = fox_gate(fdt, p["ab_fox_fbias"][i], seg=t, tb=min(m, 512))
            k_new, v_new = qkv[:, WIDTH:2 * WIDTH], qkv[:, 2 * WIDTH:]
            if decode:
                bias_new = (-ck).T.reshape(nseq, 1, t * HEADS)
                o = decode_attention("fox", _heads_to_rows(qkv[:, :WIDTH], nseq, t), _token_head_rows(k_new, nseq, t),
                                     _token_head_rows(v_new, nseq, t), p["cache_fox_k"], p["cache_fox_v"],
                                     p["page_table"], bias_new=bias_new,
                                     bias_past=fox_past_bias(p["cache_fox_logf"], p["page_table"]))
                o_fox = _rows_to_heads(o, nseq, t).astype(bf16)
                h0, c0, lv = p["state_ssd"][i].reshape(nseq, SSD_INNER, SSD_STATE), p["state_ssd_conv"][i], t
            else:
                o_fox = fox_flash(qkv, ck, tq=1024, tk=512, sub=256)
                h0 = jnp.zeros((nseq, SSD_INNER, SSD_STATE), f32)
                c0 = jnp.zeros((nseq, SSD_CONV - 1, SSD_CONV_CH), f32)
                lv = SSD_CHUNK
            y, h_new, c_new = ssd_mix(zz, uu, fdt, p["ab_conv_w"][i], p["ab_conv_b"][i], p["ab_dt_bias"][i],
                                      p["ab_A_log"][i], p["ab_D"][i], p["ab_ssd_norm"][i], h0, c0, lv=lv)
            mix_in = jnp.concatenate([o_fox, y], axis=1)
            x = matmul_residual(mix_in, p["ab_w_out"], i, x, scale=1.0, tm=tm, tn=1024, tk=1024, name="ab_out")
            st.setdefault("fox_k", []).append(k_new.reshape(nseq, t, HEADS, HEAD_DIM))
            st.setdefault("fox_v", []).append(v_new.reshape(nseq, t, HEADS, HEAD_DIM))
            st.setdefault("fox_logf", []).append(logf.reshape(nseq, t, HEADS))
            st.setdefault("ssd_conv", []).append(c_new)
            st.setdefault("ssd_state", []).append(h_new.reshape(nseq, SSD_HEADS, SSD_HEAD_DIM, SSD_STATE))
        else:
            proj = norm_matmul(x, gain, [(p["cd_w_in"], i, 0)], n=7 * WIDTH, tn=512, tm=tm, name="cd_in")
            k_new, v_new = proj[:, 5 * WIDTH:6 * WIDTH], proj[:, 6 * WIDTH:]
            if decode:
                s0, lv = p["state_hgrn"][i], t
                o = decode_attention("sb", _heads_to_rows(proj[:, 4 * WIDTH:5 * WIDTH], nseq, t),
                                     _token_head_rows(k_new, nseq, t), _token_head_rows(v_new, nseq, t),
                                     p["cache_sb_k"], p["cache_sb_v"], p["page_table"])
                o_sb = _rows_to_heads(o, nseq, t).astype(bf16)
            else:
                s0, lv = jnp.zeros((nseq, HEADS, HEAD_DIM, HEAD_DIM), f32), HG_CHUNK
                o_sb = sb_flash(proj, 4 * HEADS, tq=1024, tk=512, sub=256)
            o_hg, s_new = hgrn_mix(proj, lower_bounds[layer], p["cd_hg_norm"][i], s0, lv=lv)
            mix_in = jnp.concatenate([o_hg, o_sb], axis=1)
            x = matmul_residual(mix_in, p["cd_w_out"], i, x, scale=1.0, tm=tm, tn=1024, tk=1024, name="cd_out")
            st.setdefault("hgrn", []).append(s_new)
            st.setdefault("sb_k", []).append(k_new.reshape(nseq, t, HEADS, HEAD_DIM))
            st.setdefault("sb_v", []).append(v_new.reshape(nseq, t, HEADS, HEAD_DIM))
        qm = norm_matmul(x, p["norm_mem"][layer], [(p["mem_wq"], layer, 0)], n=MEM_WIDTH, tn=MEM_WIDTH, tm=tm,
                         name="mem_q")
        om = mem_attention(qm, mem_k, mem_v, layer, tq=min(t, 1024))
        x = matmul_residual(om, p["mem_wo"], layer, x, scale=1.0, tm=tm, tn=1024, tk=MEM_WIDTH, name="mem_out")
        hmid = norm_matmul(x, p["norm_ffn2"][layer], [(p["ffn2_gate"], layer, 0), (p["ffn2_up"], layer, 0)],
                           n=D_FF, tn=512, tm=tm, out_dtype=bf16, swiglu=True, name="ffn2_up")
        x = matmul_residual(hmid, p["ffn2_down"], layer, x, scale=0.5, tm=tm, tn=1024, tk=1408, name="ffn2_down")
    y = rmsnorm_rows(x, p["norm_final"], tm=min(m, 512))
    return y, {k: jnp.stack(v) for k, v in st.items()}


def kernel(x_prompt, x_sample, cache_fox_k, cache_fox_v, cache_fox_logf, state_ssd_conv, state_ssd, state_hgrn, cache_sb_k, cache_sb_v, cache_mem_k, cache_mem_v, page_table, mem_prompt, norm_ffn1, ffn1_gate, ffn1_up, ffn1_down, norm_mix, ab_w_in, ab_fox_fbias, ab_conv_w, ab_conv_b, ab_dt_bias, ab_A_log, ab_D, ab_ssd_norm, ab_w_out, cd_w_in, hg_lower_bound, cd_hg_norm, cd_w_out, norm_mem, norm_memkv, mem_wq, mem_wk, mem_wv, mem_wo, norm_ffn2, ffn2_gate, ffn2_up, ffn2_down, norm_final):
    p = dict(norm_ffn1=norm_ffn1, ffn1_gate=ffn1_gate, ffn1_up=ffn1_up, ffn1_down=ffn1_down, norm_mix=norm_mix,
             ab_w_in=ab_w_in, ab_fox_fbias=ab_fox_fbias, ab_conv_w=ab_conv_w, ab_conv_b=ab_conv_b,
             ab_dt_bias=ab_dt_bias, ab_A_log=ab_A_log, ab_D=ab_D, ab_ssd_norm=ab_ssd_norm, ab_w_out=ab_w_out,
             cd_w_in=cd_w_in, cd_hg_norm=cd_hg_norm, cd_w_out=cd_w_out, norm_mem=norm_mem, mem_wq=mem_wq,
             mem_wo=mem_wo, norm_ffn2=norm_ffn2, ffn2_gate=ffn2_gate, ffn2_up=ffn2_up, ffn2_down=ffn2_down,
             norm_final=norm_final, cache_fox_k=cache_fox_k, cache_fox_v=cache_fox_v, cache_fox_logf=cache_fox_logf,
             cache_sb_k=cache_sb_k, cache_sb_v=cache_sb_v, page_table=page_table, state_ssd=state_ssd,
             state_ssd_conv=state_ssd_conv, state_hgrn=state_hgrn)
    depth = norm_ffn1.shape[0]
    probs = jax.nn.softmax(hg_lower_bound.astype(f32), axis=0)
    lower_bounds = jnp.cumsum(probs, axis=0) - probs[0]

    bp, seq, _ = x_prompt.shape
    bs, dseq, _ = x_sample.shape
    mem_len = mem_prompt.shape[1]
    mem_rows = mem_prompt.reshape(bp * mem_len, D_MODEL)
    mk = [norm_matmul(mem_rows, norm_memkv[l], [(mem_wk, l, 0)], n=MEM_WIDTH, tn=MEM_WIDTH, tm=mem_len, name="mem_k")
          for l in range(depth)]
    mv = [norm_matmul(mem_rows, norm_memkv[l], [(mem_wv, l, 0)], n=MEM_WIDTH, tn=MEM_WIDTH, tm=mem_len, name="mem_v")
          for l in range(depth)]
    mem_k_p = jnp.stack(mk).reshape(depth, bp, mem_len, MEM_WIDTH)
    mem_v_p = jnp.stack(mv).reshape(depth, bp, mem_len, MEM_WIDTH)

    y_p, sp = _trunk(x_prompt.reshape(bp * seq, D_MODEL), mem_k_p, mem_v_p, p, lower_bounds, nseq=bp, decode=False)
    y_s, ss = _trunk(x_sample.reshape(bs * dseq, D_MODEL), cache_mem_k.reshape(depth, bs, mem_len, MEM_WIDTH),
                     cache_mem_v.reshape(depth, bs, mem_len, MEM_WIDTH), p, lower_bounds, nseq=bs, decode=True)

    m5 = (depth, bp, mem_len, MEM_HEADS, HEAD_DIM)
    return (y_p.reshape(bp, seq, D_MODEL), y_s.reshape(bs, dseq, D_MODEL),
            sp["fox_k"], sp["fox_v"], sp["fox_logf"], sp["ssd_conv"], sp["ssd_state"], sp["hgrn"], sp["sb_k"], sp["sb_v"],
            mem_k_p.reshape(m5), mem_v_p.reshape(m5),
            ss["fox_k"], ss["fox_v"], ss["fox_logf"], ss["ssd_conv"], ss["ssd_state"], ss["hgrn"], ss["sb_k"], ss["sb_v"])
```

```python
import functools
import math

import numpy as np
import jax
import jax.numpy as jnp
from jax import lax
from jax.experimental import pallas as pl
from jax.experimental.pallas import tpu as pltpu

f32 = jnp.float32
bf16 = jnp.bfloat16
HIGHEST = lax.Precision.HIGHEST

D_MODEL = 2048
D_FF = 5632
HEADS = 8
HEAD_DIM = 128
WIDTH = HEADS * HEAD_DIM
SSD_HEADS = 32
SSD_HEAD_DIM = 64
SSD_INNER = SSD_HEADS * SSD_HEAD_DIM
SSD_GROUPS = 4
SSD_STATE = 128
SSD_CONV = 4
SSD_CONV_CH = SSD_INNER + 2 * SSD_GROUPS * SSD_STATE
SSD_CHUNK = 128
HG_CHUNK = 64
HG_SUB = 16
MEM_HEADS = 4
MEM_WIDTH = MEM_HEADS * HEAD_DIM
PAGE = 128
NORM_EPS = 1e-6
NEG_INF = -1e30
ATTN_SCALE = HEAD_DIM ** -0.5
LOG2E = math.log2(math.e)

LANES = 128
SUBLANES = 8
VMEM_LIMIT_BYTES = 56 * 1024 * 1024
FDT_LANE0 = 8

NT_DIMS = (((1,), (1,)), ((), ()))
TN_DIMS = (((0,), (0,)), ((), ()))


def _params(sem):
    return pltpu.CompilerParams(dimension_semantics=sem, vmem_limit_bytes=VMEM_LIMIT_BYTES)


def _dot(a, b):
    return jnp.dot(a, b, preferred_element_type=f32)


def _dot_nt(a, b):
    return lax.dot_general(a, b, NT_DIMS, preferred_element_type=f32)


def _dot_tn(a, b):
    return lax.dot_general(a, b, TN_DIMS, preferred_element_type=f32)


def _split_dot(x, e):
    hi = x.astype(bf16)
    lo = (x - hi.astype(f32)).astype(bf16)
    return _dot(jnp.concatenate([hi, lo], axis=1), jnp.concatenate([e, e], axis=0))


def _silu(x):
    return x * jax.nn.sigmoid(x)


def _log_sigmoid(x):
    return jnp.minimum(x, 0.0) - jnp.log(1.0 + jnp.exp(-jnp.abs(x)))


def _nmm_kernel(x_ref, g_ref, *refs, n_w, swiglu):
    w_refs, o_ref, xn_ref = refs[:n_w], refs[n_w], refs[n_w + 1]

    @pl.when(pl.program_id(1) == 0)
    def _():
        x = x_ref[...]
        ms = jnp.mean(x * x, axis=-1, keepdims=True)
        xn_ref[...] = (x * lax.rsqrt(ms + NORM_EPS) * g_ref[...]).astype(bf16)

    xn = xn_ref[...]
    if swiglu:
        g = _dot(xn, w_refs[0][...].astype(bf16))
        u = _dot(xn, w_refs[1][...].astype(bf16))
        o_ref[...] = (_silu(g) * u).astype(o_ref.dtype)
    else:
        o_ref[...] = _dot(xn, w_refs[0][...].astype(bf16)).astype(o_ref.dtype)


def _w_spec(w, lead, k, tn, off):
    if w.ndim == 3:
        return pl.BlockSpec((None, k, tn), lambda i, j: (lead, 0, j + off))
    return pl.BlockSpec((k, tn), lambda i, j: (0, j + off))


def norm_matmul(x, gain, ws, *, n, tn, tm, out_dtype=f32, swiglu=False, name="nmm"):
    m, k = x.shape
    assert m % tm == 0 and n % tn == 0
    in_specs = [pl.BlockSpec((tm, k), lambda i, j: (i, 0)), pl.BlockSpec((1, k), lambda i, j: (0, 0))]
    in_specs += [_w_spec(w, lead, k, tn, off) for (w, lead, off) in ws]
    return pl.pallas_call(
        functools.partial(_nmm_kernel, n_w=len(ws), swiglu=swiglu),
        out_shape=jax.ShapeDtypeStruct((m, n), out_dtype),
        grid=(m // tm, n // tn),
        in_specs=in_specs,
        out_specs=pl.BlockSpec((tm, tn), lambda i, j: (i, j)),
        scratch_shapes=[pltpu.VMEM((tm, k), bf16)],
        compiler_params=_params(("arbitrary", "arbitrary")),
        name=name,
    )(x, gain.reshape(1, k), *[w for (w, _, _) in ws])


def _mmr_kernel(a_ref, w_ref, r_ref, o_ref, acc_ref, *, nk, scale):
    k = pl.program_id(2)
    p = _dot(a_ref[...].astype(bf16), w_ref[...].astype(bf16))

    @pl.when(k == 0)
    def _():
        acc_ref[...] = p

    @pl.when(k > 0)
    def _():
        acc_ref[...] += p

    @pl.when(k == nk - 1)
    def _():
        o_ref[...] = r_ref[...] + scale * acc_ref[...]


def matmul_residual(a, w, lead, res, *, scale, tm, tn, tk, name="mmr"):
    m, kk = a.shape
    n = res.shape[1]
    assert m % tm == 0 and n % tn == 0 and kk % tk == 0
    nk = kk // tk
    if w.ndim == 3:
        w_spec = pl.BlockSpec((None, tk, tn), lambda i, j, k: (lead, k, j))
    else:
        w_spec = pl.BlockSpec((tk, tn), lambda i, j, k: (k, j))
    return pl.pallas_call(
        functools.partial(_mmr_kernel, nk=nk, scale=scale),
        out_shape=jax.ShapeDtypeStruct((m, n), f32),
        grid=(m // tm, n // tn, nk),
        in_specs=[pl.BlockSpec((tm, tk), lambda i, j, k: (i, k)), w_spec,
                  pl.BlockSpec((tm, tn), lambda i, j, k: (i, j))],
        out_specs=pl.BlockSpec((tm, tn), lambda i, j, k: (i, j)),
        scratch_shapes=[pltpu.VMEM((tm, tn), f32)],
        compiler_params=_params(("arbitrary", "arbitrary", "arbitrary")),
        name=name,
    )(a, w, res)


def _rmsnorm_kernel(x_ref, g_ref, o_ref):
    x = x_ref[...]
    ms = jnp.mean(x * x, axis=-1, keepdims=True)
    o_ref[...] = x * lax.rsqrt(ms + NORM_EPS) * g_ref[...]


def rmsnorm_rows(x, gain, *, tm):
    m, k = x.shape
    return pl.pallas_call(
        _rmsnorm_kernel,
        out_shape=jax.ShapeDtypeStruct((m, k), f32),
        grid=(m // tm,),
        in_specs=[pl.BlockSpec((tm, k), lambda i: (i, 0)), pl.BlockSpec((1, k), lambda i: (0, 0))],
        out_specs=pl.BlockSpec((tm, k), lambda i: (i, 0)),
        compiler_params=_params(("arbitrary",)),
        name="final_norm",
    )(x, gain.reshape(1, k))


def _fox_gate_kernel(fdt_ref, bias_ref, u_ref, lf_ref, ck_ref, carry_ref, *, tb, seg, carry):
    lf = _log_sigmoid(fdt_ref[...] + bias_ref[...])
    lf_ref[...] = lf[:, :HEADS]
    lft = lf.T[:HEADS, :]
    cum = jnp.dot(lft, u_ref[...], preferred_element_type=f32, precision=HIGHEST)
    if carry:
        @pl.when((pl.program_id(0) * tb) % seg == 0)
        def _():
            carry_ref[...] = jnp.zeros_like(carry_ref)
        cum = cum + carry_ref[:, 0:1]
        carry_ref[...] = jnp.broadcast_to(cum[:, tb - 1:tb], carry_ref.shape)
    ck_ref[...] = cum


def fox_gate(fdt, fbias, *, seg, tb):
    m = fdt.shape[0]
    assert m % tb == 0 and (seg % tb == 0 or tb % seg == 0)
    r = np.arange(tb)
    u = ((r[:, None] <= r[None, :]) & (r[:, None] // seg == r[None, :] // seg)).astype(np.float32)
    bias = jnp.zeros((1, LANES), f32).at[0, :HEADS].set(fbias)
    return pl.pallas_call(
        functools.partial(_fox_gate_kernel, tb=tb, seg=seg, carry=seg > tb),
        out_shape=(jax.ShapeDtypeStruct((m, HEADS), f32), jax.ShapeDtypeStruct((HEADS, m), f32)),
        grid=(m // tb,),
        in_specs=[pl.BlockSpec((tb, LANES), lambda i: (i, 0)), pl.BlockSpec((1, LANES), lambda i: (0, 0)),
                  pl.BlockSpec((tb, tb), lambda i: (0, 0))],
        out_specs=(pl.BlockSpec((tb, HEADS), lambda i: (i, 0)), pl.BlockSpec((HEADS, tb), lambda i: (0, i))),
        scratch_shapes=[pltpu.VMEM((HEADS, LANES), f32)],
        compiler_params=_params(("arbitrary",)),
        name="fox_gate",
    )(fdt, bias, jnp.asarray(u))


def _lanes(x, width):
    if width <= LANES:
        return x[:, :width]
    return jnp.concatenate([x] * (width // LANES), axis=1)


def _softmax_step(s, vb, state):
    m_prev, l_prev, acc = state
    m_new = jnp.maximum(m_prev, jnp.max(s, axis=-1, keepdims=True))
    alpha = jnp.exp2(m_prev - m_new)
    p = jnp.exp2(s - _lanes(m_new, s.shape[1]))
    l_new = alpha * l_prev + jnp.sum(p, axis=-1, keepdims=True)
    return m_new, l_new, alpha * acc + _dot(p.astype(bf16), vb)


def _softmax_update(s, vb, m_ref, l_ref, acc_ref, rows):
    m_ref[rows, :], l_ref[rows, :], acc_ref[rows, :] = _softmax_step(
        s, vb, (m_ref[rows, :], l_ref[rows, :], acc_ref[rows, :]))


def _fox_flash_kernel(q_ref, k_ref, v_ref, ck_ref, o_ref, qs_ref, m_ref, l_ref, acc_ref, *, tq, tk, sub):
    i = pl.program_id(1)
    nkb = tq // tk
    qs_ref[...] = (q_ref[...] * (ATTN_SCALE * LOG2E)).astype(bf16)
    m_ref[...] = jnp.full(m_ref.shape, NEG_INF, f32)
    l_ref[...] = jnp.zeros(l_ref.shape, f32)
    acc_ref[...] = jnp.zeros(acc_ref.shape, f32)

    def step(j, diag):
        start = pl.multiple_of(j * tk, tk)
        kb = k_ref[pl.ds(start, tk), :].astype(bf16)
        vb = v_ref[pl.ds(start, tk), :].astype(bf16)
        bias = ck_ref[:, pl.ds(start, tk)]
        for r0 in range(0, tq, sub):
            masked = False
            if diag is not None:
                if diag * tk > r0 + sub - 1:
                    continue
                masked = diag * tk + tk - 1 > r0
            rows = slice(r0, r0 + sub)
            s = _dot_nt(qs_ref[rows, :], kb) - bias
            if masked:
                row = lax.broadcasted_iota(jnp.int32, (sub, tk), 0) + r0
                col = lax.broadcasted_iota(jnp.int32, (sub, tk), 1) + diag * tk
                s = jnp.where(col <= row, s, NEG_INF)
            _softmax_update(s, vb, m_ref, l_ref, acc_ref, rows)

    def body(j, c):
        step(j, None)
        return c

    lax.fori_loop(0, i * nkb, body, 0)
    for d in range(nkb):
        step(i * nkb + d, d)
    o_ref[...] = (acc_ref[...] / l_ref[...]).astype(o_ref.dtype)


def fox_flash(qkv, ck, *, tq, tk, sub):
    t = qkv.shape[0]
    assert t % tq == 0 and tq % tk == 0 and tq % sub == 0 and tk % LANES == 0
    stat = pltpu.VMEM((tq, LANES), f32)
    return pl.pallas_call(
        functools.partial(_fox_flash_kernel, tq=tq, tk=tk, sub=sub),
        out_shape=jax.ShapeDtypeStruct((t, WIDTH), bf16),
        grid=(HEADS, t // tq),
        in_specs=[pl.BlockSpec((tq, HEAD_DIM), lambda h, i: (i, h)),
                  pl.BlockSpec((t, HEAD_DIM), lambda h, i: (0, HEADS + h)),
                  pl.BlockSpec((t, HEAD_DIM), lambda h, i: (0, 2 * HEADS + h)),
                  pl.BlockSpec((None, 1, t), lambda h, i: (h, 0, 0))],
        out_specs=pl.BlockSpec((tq, HEAD_DIM), lambda h, i: (i, h)),
        scratch_shapes=[pltpu.VMEM((tq, HEAD_DIM), bf16), stat, stat, pltpu.VMEM((tq, HEAD_DIM), f32)],
        compiler_params=_params(("arbitrary", "arbitrary")),
        name="fox_flash",
    )(qkv, qkv, qkv, (ck * LOG2E).reshape(HEADS, 1, t))


def _sb_tile(z, valid, tri_ext, r):
    nr, sk = z.shape
    cw = tri_ext.shape[0]
    lk = -(jnp.maximum(z, 0.0) + jnp.log2(1.0 + jnp.exp2(-jnp.abs(z))))
    if valid is not None:
        lk = jnp.where(valid, lk, 0.0)
    nch = sk // cw
    stacked = lk if nch == 1 else jnp.concatenate([lk[:, c * cw:(c + 1) * cw] for c in range(nch)], axis=0)
    inc = _split_dot(stacked, tri_ext)
    sums = [None] * nch
    for c in range(nch - 1, -1, -1):
        blk = inc[c * nr:(c + 1) * nr]
        sums[c] = blk[:, :cw] + _lanes(r, cw)
        r = r + blk[:, cw:]
    w = jnp.exp2(z + (sums[0] if nch == 1 else jnp.concatenate(sums, axis=1)))
    if valid is not None:
        w = jnp.where(valid, w, 0.0)
    return w, r


def _sb_flash_kernel(q_ref, k_ref, v_ref, tri_ref, o_ref, qs_ref, r_ref, acc_ref, *, tq, tk, sub):
    i = pl.program_id(1)
    nkb = tq // tk
    qs_ref[...] = (q_ref[...] * (ATTN_SCALE * LOG2E)).astype(bf16)
    r_ref[...] = jnp.zeros(r_ref.shape, f32)
    acc_ref[...] = jnp.zeros(acc_ref.shape, f32)

    def step(j, diag):
        start = pl.multiple_of(j * tk, tk)
        kb = k_ref[pl.ds(start, tk), :].astype(bf16)
        vb = v_ref[pl.ds(start, tk), :].astype(bf16)
        for r0 in range(0, tq, sub):
            valid = None
            if diag is not None:
                if diag * tk >= r0 + sub - 1:
                    continue
                if diag * tk + tk - 1 >= r0:
                    row = lax.broadcasted_iota(jnp.int32, (sub, tk), 0) + r0
                    col = lax.broadcasted_iota(jnp.int32, (sub, tk), 1) + diag * tk
                    valid = col < row
            rows = slice(r0, r0 + sub)
            w, r_new = _sb_tile(_dot_nt(qs_ref[rows, :], kb), valid, tri_ref[...], r_ref[rows, :])
            acc_ref[rows, :] += _dot(w.astype(bf16), vb)
            r_ref[rows, :] = r_new

    for d in range(nkb - 1, -1, -1):
        step(i * nkb + d, d)

    def body(t, c):
        step(i * nkb - 1 - t, None)
        return c

    lax.fori_loop(0, i * nkb, body, 0)
    o_ref[...] = acc_ref[...].astype(o_ref.dtype)


def _tri_ext(n):
    r = np.arange(n)
    tri = (r[:, None] >= r[None, :]).astype(np.float32)
    return jnp.asarray(np.concatenate([tri, np.ones((n, LANES), np.float32)], axis=1), dtype=bf16)


def sb_flash(proj, col0, *, tq, tk, sub):
    t = proj.shape[0]
    assert t % tq == 0 and tq % tk == 0 and tq % sub == 0 and tk % LANES == 0
    return pl.pallas_call(
        functools.partial(_sb_flash_kernel, tq=tq, tk=tk, sub=sub),
        out_shape=jax.ShapeDtypeStruct((t, WIDTH), bf16),
        grid=(HEADS, t // tq),
        in_specs=[pl.BlockSpec((tq, HEAD_DIM), lambda h, i: (i, col0 + h)),
                  pl.BlockSpec((t, HEAD_DIM), lambda h, i: (0, col0 + HEADS + h)),
                  pl.BlockSpec((t, HEAD_DIM), lambda h, i: (0, col0 + 2 * HEADS + h)),
                  pl.BlockSpec((LANES, 2 * LANES), lambda h, i: (0, 0))],
        out_specs=pl.BlockSpec((tq, HEAD_DIM), lambda h, i: (i, h)),
        scratch_shapes=[pltpu.VMEM((tq, HEAD_DIM), bf16), pltpu.VMEM((tq, LANES), f32),
                        pltpu.VMEM((tq, HEAD_DIM), f32)],
        compiler_params=_params(("arbitrary", "arbitrary")),
        name="sb_flash",
    )(proj, proj, proj, _tri_ext(LANES))


PAGE_TILE_ROWS = PAGE * HEADS // LANES
TOK_PER_ROW = LANES // HEADS


def _past_bias_kernel(pt_ref, lf_hbm, dm_ref, hm_ref, up_ref, o_ref, buf, sem, *, n_pages):
    b = pl.program_id(0)

    def copy(j):
        return pltpu.make_async_copy(lf_hbm.at[pt_ref[b, j]], buf.at[j], sem)

    def start(j, c):
        copy(j).start()
        return c

    def wait(j, c):
        copy(j).wait()
        return c

    lax.fori_loop(0, n_pages, start, 0)
    lax.fori_loop(0, n_pages, wait, 0)

    def hdot(a, m_ref):
        return jnp.dot(a, m_ref[...], preferred_element_type=f32, precision=HIGHEST)

    ys = [buf[:, r, :] for r in range(PAGE_TILE_ROWS)]
    later = [None] * PAGE_TILE_ROWS
    acc = jnp.zeros_like(ys[0])
    for r in range(PAGE_TILE_ROWS - 1, -1, -1):
        later[r] = acc
        acc = acc + ys[r]
    pages_after = hdot(jnp.dot(up_ref[...], acc, preferred_element_type=f32, precision=HIGHEST), hm_ref)
    for r in range(PAGE_TILE_ROWS):
        o_ref[:, r, :] = hdot(ys[r], dm_ref) + hdot(later[r], hm_ref) + pages_after


def fox_past_bias(cache_logf, page_table):
    nb, n_pages = page_table.shape
    n_pool = cache_logf.shape[1]
    lane = np.arange(LANES)
    same_head = lane[:, None] % HEADS == lane[None, :] % HEADS
    dm = (same_head & (lane[:, None] // HEADS > lane[None, :] // HEADS)).astype(np.float32)
    hm = same_head.astype(np.float32)
    pg = np.arange(n_pages)
    up = (pg[None, :] > pg[:, None]).astype(np.float32)
    const = lambda shape: pl.BlockSpec(shape, lambda b, pt: (0, 0))
    return pl.pallas_call(
        functools.partial(_past_bias_kernel, n_pages=n_pages),
        out_shape=jax.ShapeDtypeStruct((nb, n_pages, PAGE_TILE_ROWS, LANES), f32),
        grid_spec=pltpu.PrefetchScalarGridSpec(
            num_scalar_prefetch=1,
            grid=(nb,),
            in_specs=[pl.BlockSpec(memory_space=pl.ANY), const((LANES, LANES)), const((LANES, LANES)),
                      const((n_pages, n_pages))],
            out_specs=pl.BlockSpec((None, n_pages, PAGE_TILE_ROWS, LANES), lambda b, pt: (b, 0, 0, 0)),
            scratch_shapes=[pltpu.VMEM((n_pages, PAGE_TILE_ROWS, LANES), f32), pltpu.SemaphoreType.DMA(())],
        ),
        compiler_params=_params(("arbitrary",)),
        name="fox_past_bias",
    )(page_table, cache_logf.reshape(n_pool, PAGE_TILE_ROWS, LANES), jnp.asarray(dm), jnp.asarray(hm), jnp.asarray(up))


def _decode_kernel(pt_ref, q_ref, kn_ref, vn_ref, hm_ref, *refs, mode, pp, n_new):
    k_refs, v_refs = refs[:pp], refs[pp:2 * pp]
    refs = refs[2 * pp:]
    if mode == "fox":
        bn_ref, ap_refs = refs[0], refs[1:1 + pp]
        o_ref = refs[1 + pp]
        state_refs = refs[2 + pp:]
    else:
        tri_ref = refs[0]
        o_ref = refs[1]
        state_refs = refs[2:]
    j = pl.program_id(1)
    rows = HEADS * n_new
    page_cols = PAGE * HEADS
    qs = (q_ref[...] * (ATTN_SCALE * LOG2E)).astype(bf16)

    def own_head(cols):
        rh = lax.broadcasted_iota(jnp.int32, (rows, cols), 0) // n_new
        ch = lax.broadcasted_iota(jnp.int32, (rows, cols), 1) % HEADS
        return rh == ch

    def fox_block(kbs, vbs, biases, valid, state):
        m_prev, l_prev, acc = state
        ss = [jnp.where(valid, _dot_nt(qs, kb) + bias, NEG_INF) for kb, bias in zip(kbs, biases)]
        m_new = m_prev
        for s in ss:
            m_new = jnp.maximum(m_new, jnp.max(s, axis=-1, keepdims=True))
        alpha = jnp.exp2(m_prev - m_new)
        l_new, acc = alpha * l_prev, alpha * acc
        for s, vb in zip(ss, vbs):
            p = jnp.exp2(s - _lanes(m_new, s.shape[1]))
            l_new = l_new + jnp.sum(p, axis=-1, keepdims=True)
            acc = acc + _dot(p.astype(bf16), vb)
        return m_new, l_new, acc

    def sb_block(kbs, vbs, valid, state):
        r, acc = state
        cols = kbs[0].shape[0]
        cw = min(cols, LANES)
        tri = tri_ref[...] if cw == LANES else jnp.concatenate([tri_ref[:cw, :cw], tri_ref[:cw, LANES:]], axis=1)
        n = len(kbs)
        z = jnp.concatenate([_dot_nt(qs, kb) for kb in kbs], axis=1) if n > 1 else _dot_nt(qs, kbs[0])
        w, r = _sb_tile(z, jnp.concatenate([valid] * n, axis=1) if n > 1 else valid, tri, r)
        for p, vb in enumerate(vbs):
            acc = acc + _dot(w[:, p * cols:(p + 1) * cols].astype(bf16), vb)
        return r, acc

    @pl.when(j == 0)
    def _():
        cols = n_new * HEADS
        tq = lax.broadcasted_iota(jnp.int32, (rows, cols), 0) % n_new
        sk = lax.broadcasted_iota(jnp.int32, (rows, cols), 1) // HEADS
        kb, vb = kn_ref[...].astype(bf16), vn_ref[...].astype(bf16)
        zeros = jnp.zeros((rows, LANES), f32)
        if mode == "fox":
            state = fox_block([kb], [vb], [bn_ref[...] * LOG2E], own_head(cols) & (sk <= tq),
                              (jnp.full((rows, LANES), NEG_INF, f32), zeros, zeros))
        else:
            state = sb_block([kb], [vb], own_head(cols) & (sk < tq), (zeros, zeros))
        for ref, val in zip(state_refs, state):
            ref[...] = val

    kbs = [k_refs[p][...].reshape(page_cols, HEAD_DIM).astype(bf16) for p in range(pp)]
    vbs = [v_refs[p][...].reshape(page_cols, HEAD_DIM).astype(bf16) for p in range(pp)]
    state = tuple(ref[...] for ref in state_refs)
    if mode == "fox":
        biases = []
        for p in range(pp):
            ap = ap_refs[p][...] * LOG2E
            biases.append(jnp.concatenate(
                [jnp.broadcast_to(ap[r:r + 1], (rows, LANES)) for r in range(PAGE_TILE_ROWS)], axis=1))
        state = fox_block(kbs, vbs, biases, hm_ref[...] > 0.5, state)
    else:
        state = sb_block(kbs, vbs, hm_ref[...] > 0.5, state)
    for ref, val in zip(state_refs, state):
        ref[...] = val

    @pl.when(j == pl.num_programs(1) - 1)
    def _():
        if mode == "fox":
            o_ref[...] = state[2] / state[1]
        else:
            o_ref[...] = state[1]


def decode_attention(mode, q, k_new, v_new, cache_k, cache_v, page_table, *, bias_new=None, bias_past=None, pp=8):
    nb, rows, _ = q.shape
    n_new = rows // HEADS
    n_pages = page_table.shape[1]
    assert n_pages % pp == 0
    nsteps = n_pages // pp

    def page_of(b, j, pt, p):
        return pt[b, (nsteps - 1 - j) * pp + p]

    kv_specs = [pl.BlockSpec((None, None, PAGE, HEADS, HEAD_DIM),
                             lambda b, j, pt, p=p: (0, page_of(b, j, pt, p), 0, 0, 0)) for p in range(pp)]
    new_spec = pl.BlockSpec((None, rows, HEAD_DIM), lambda b, j, pt: (b, 0, 0))
    own_head = (np.arange(rows)[:, None] // n_new == np.arange(PAGE * HEADS)[None, :] % HEADS).astype(np.float32)
    in_specs = [new_spec, new_spec, new_spec, pl.BlockSpec((rows, PAGE * HEADS), lambda b, j, pt: (0, 0))]
    in_specs += kv_specs + kv_specs
    args = [q, k_new, v_new, jnp.asarray(own_head)] + [cache_k] * pp + [cache_v] * pp
    stat = pltpu.VMEM((rows, LANES), f32)
    if mode == "fox":
        in_specs.append(pl.BlockSpec((None, 1, rows), lambda b, j, pt: (b, 0, 0)))
        in_specs += [pl.BlockSpec((None, None, PAGE_TILE_ROWS, LANES),
                                  lambda b, j, pt, p=p: (b, (nsteps - 1 - j) * pp + p, 0, 0)) for p in range(pp)]
        args += [bias_new] + [bias_past] * pp
        scratch = [stat, stat, pltpu.VMEM((rows, HEAD_DIM), f32)]
    else:
        in_specs.append(pl.BlockSpec((LANES, 2 * LANES), lambda b, j, pt: (0, 0)))
        args.append(_tri_ext(LANES))
        scratch = [stat, pltpu.VMEM((rows, HEAD_DIM), f32)]
    return pl.pallas_call(
        functools.partial(_decode_kernel, mode=mode, pp=pp, n_new=n_new),
        out_shape=jax.ShapeDtypeStruct((nb, rows, HEAD_DIM), f32),
        grid_spec=pltpu.PrefetchScalarGridSpec(
            num_scalar_prefetch=1,
            grid=(nb, nsteps),
            in_specs=in_specs,
            out_specs=pl.BlockSpec((None, rows, HEAD_DIM), lambda b, j, pt: (b, 0, 0)),
            scratch_shapes=scratch,
        ),
        compiler_params=_params(("arbitrary", "arbitrary")),
        name=mode + "_decode",
    )(page_table, *args)


def _mem_attn_kernel(q_ref, k_ref, v_ref, o_ref):
    s = _dot_nt(q_ref[...].astype(bf16), k_ref[...].astype(bf16)) * ATTN_SCALE
    m = jnp.max(s, axis=-1, keepdims=True)
    p = jnp.exp(s - m)
    l = jnp.sum(p, axis=-1, keepdims=True)
    o_ref[...] = (_dot(p.astype(bf16), v_ref[...].astype(bf16)) / l).astype(o_ref.dtype)


def mem_attention(q, mem_k, mem_v, layer, *, tq):
    m = q.shape[0]
    nb, mem_len = mem_k.shape[1], mem_k.shape[2]
    per_b = m // nb
    assert per_b % tq == 0
    nq = per_b // tq
    kv_spec = pl.BlockSpec((None, None, mem_len, HEAD_DIM), lambda i, h: (layer, i // nq, 0, h))
    return pl.pallas_call(
        _mem_attn_kernel,
        out_shape=jax.ShapeDtypeStruct((m, MEM_WIDTH), bf16),
        grid=(m // tq, MEM_HEADS),
        in_specs=[pl.BlockSpec((tq, HEAD_DIM), lambda i, h: (i, h)), kv_spec, kv_spec],
        out_specs=pl.BlockSpec((tq, HEAD_DIM), lambda i, h: (i, h)),
        compiler_params=_params(("arbitrary", "arbitrary")),
        name="mem_attn",
    )(q, mem_k, mem_v)


def _ssd_kernel(z_ref, u_ref, fdt_ref, cw_ref, cb_ref, dtb_ref, alog_ref, dexp_ref, gn_ref, e_ref, tri_ref,
                h0_ref, c0_ref, y_ref, hout_ref, cout_ref, ht_ref, cbuf_ref, *, ll, lv):
    c = pl.program_id(1)
    npad = SSD_CONV - 1
    base = SUBLANES
    hpg = SSD_HEADS // SSD_GROUPS
    gw = hpg * SSD_HEAD_DIM

    @pl.when(c == 0)
    def _():
        ht_ref[...] = h0_ref[...].T
        cbuf_ref[base - npad:base, :] = c0_ref[...]
        if lv < ll:
            cbuf_ref[base:base + ll, :] = jnp.zeros((ll, SSD_CONV_CH), f32)

    cbuf_ref[base:base + lv, :] = u_ref[...]
    xbc = cb_ref[...] + sum(cw_ref[i:i + 1, :] * cbuf_ref[base - npad + i:base - npad + i + ll, :]
                            for i in range(SSD_CONV))
    conv_tail = cbuf_ref[base + lv - npad:base + lv, :]
    cbuf_ref[base - npad:base, :] = conv_tail
    xbc = _silu(xbc)

    fdt = fdt_ref[...] if lv == ll else jnp.concatenate([fdt_ref[...], jnp.zeros((ll - lv, LANES), f32)], axis=0)
    dt = jax.nn.softplus(fdt + dtb_ref[...])
    lane = lax.broadcasted_iota(jnp.int32, (ll, LANES), 1)
    row = lax.broadcasted_iota(jnp.int32, (ll, LANES), 0)
    live = (lane >= FDT_LANE0) & (lane < FDT_LANE0 + SSD_HEADS) & (row < lv)
    dt = jnp.where(live, dt, 0.0)
    xs = xbc[:, :SSD_INNER]
    if lv < ll:
        xs = jnp.where(lax.broadcasted_iota(jnp.int32, (ll, SSD_INNER), 0) < lv, xs, 0.0)
    bm = xbc[:, SSD_INNER:SSD_INNER + SSD_GROUPS * SSD_STATE].astype(bf16)
    cm = xbc[:, SSD_INNER + SSD_GROUPS * SSD_STATE:].astype(bf16)

    dta = dt * (-jnp.exp(alog_ref[...]))
    cum = jnp.dot(tri_ref[...], dta, preferred_element_type=f32, precision=HIGHEST)
    cum_t = cum.T
    dt_t = dt.T
    cum_last = cum[ll - 1:ll, :]
    e = e_ref[...]
    ecx = _split_dot(jnp.exp(cum), e)
    tex = _split_dot(jnp.exp(cum_last - cum) * dt, e)
    causal = lax.broadcasted_iota(jnp.int32, (ll, ll), 0) >= lax.broadcasted_iota(jnp.int32, (ll, ll), 1)

    ys = []
    for g in range(SSD_GROUPS):
        cg = cm[:, g * SSD_STATE:(g + 1) * SSD_STATE]
        bg = bm[:, g * SSD_STATE:(g + 1) * SSD_STATE]
        cbm = _dot_nt(cg, bg)
        xg = xs[:, g * gw:(g + 1) * gw]
        htg = ht_ref[:, g * gw:(g + 1) * gw]
        y_parts = []
        for j in range(hpg):
            ln = FDT_LANE0 + g * hpg + j
            seg = cum[:, ln:ln + 1] - cum_t[ln:ln + 1, :]
            decay = jnp.where(causal, jnp.exp(jnp.where(causal, seg, 0.0)), 0.0)
            mh = (cbm * decay * dt_t[ln:ln + 1, :]).astype(bf16)
            y_parts.append(_dot(mh, xg[:, j * SSD_HEAD_DIM:(j + 1) * SSD_HEAD_DIM].astype(bf16)))
        yg = jnp.concatenate(y_parts, axis=1) + _dot(cg, htg.astype(bf16)) * ecx[:, g * gw:(g + 1) * gw]
        ys.append(yg)
        xp = (xg * tex[:, g * gw:(g + 1) * gw]).astype(bf16)
        ht_ref[:, g * gw:(g + 1) * gw] = ecx[ll - 1:ll, g * gw:(g + 1) * gw] * htg + _dot_tn(bg, xp)

    y = jnp.concatenate(ys, axis=1) + dexp_ref[...] * xs
    zz = z_ref[...] if lv == ll else jnp.concatenate([z_ref[...], jnp.zeros((ll - lv, SSD_INNER), f32)], axis=0)
    y = y * _silu(zz)
    outs = []
    for g in range(SSD_GROUPS):
        yg = y[:, g * gw:(g + 1) * gw]
        ms = jnp.mean(yg * yg, axis=-1, keepdims=True)
        outs.append(yg * lax.rsqrt(ms + NORM_EPS))
    yn = jnp.concatenate(outs, axis=1) * gn_ref[...]
    y_ref[...] = yn[:lv].astype(y_ref.dtype)

    @pl.when(c == pl.num_programs(1) - 1)
    def _():
        hout_ref[...] = ht_ref[...].T
        cout_ref[...] = conv_tail


def ssd_mix(z, u, fdt, conv_w, conv_b, dt_bias, a_log, d_skip, gnorm, h0, c0, *, lv):
    m = z.shape[0]
    nseq = h0.shape[0]
    ll = SSD_CHUNK
    assert lv <= ll and lv % SUBLANES == 0 and lv >= SSD_CONV - 1 and m % (nseq * lv) == 0
    nchunk = m // (nseq * lv)
    pad = lambda v: jnp.zeros((1, LANES), f32).at[0, FDT_LANE0:FDT_LANE0 + SSD_HEADS].set(v)
    e = np.zeros((LANES, SSD_INNER), np.float32)
    for h in range(SSD_HEADS):
        e[FDT_LANE0 + h, h * SSD_HEAD_DIM:(h + 1) * SSD_HEAD_DIM] = 1.0
    r = np.arange(ll)
    tri = jnp.asarray((r[:, None] >= r[None, :]).astype(np.float32))
    row = lambda i, c: (i * nchunk + c, 0)
    const = lambda i, c: (0, 0)
    seq3 = lambda i, c: (i, 0, 0)
    return pl.pallas_call(
        functools.partial(_ssd_kernel, ll=ll, lv=lv),
        out_shape=(jax.ShapeDtypeStruct((m, SSD_INNER), bf16),
                   jax.ShapeDtypeStruct((nseq, SSD_INNER, SSD_STATE), f32),
                   jax.ShapeDtypeStruct((nseq, SSD_CONV - 1, SSD_CONV_CH), f32)),
        grid=(nseq, nchunk),
        in_specs=[pl.BlockSpec((lv, SSD_INNER), row), pl.BlockSpec((lv, SSD_CONV_CH), row),
                  pl.BlockSpec((lv, LANES), row),
                  pl.BlockSpec((SSD_CONV, SSD_CONV_CH), const), pl.BlockSpec((1, SSD_CONV_CH), const),
                  pl.BlockSpec((1, LANES), const), pl.BlockSpec((1, LANES), const),
                  pl.BlockSpec((1, SSD_INNER), const), pl.BlockSpec((1, SSD_INNER), const),
                  pl.BlockSpec((LANES, SSD_INNER), const), pl.BlockSpec((ll, ll), const),
                  pl.BlockSpec((None, SSD_INNER, SSD_STATE), seq3),
                  pl.BlockSpec((None, SSD_CONV - 1, SSD_CONV_CH), seq3)],
        out_specs=(pl.BlockSpec((lv, SSD_INNER), row),
                   pl.BlockSpec((None, SSD_INNER, SSD_STATE), seq3),
                   pl.BlockSpec((None, SSD_CONV - 1, SSD_CONV_CH), seq3)),
        scratch_shapes=[pltpu.VMEM((SSD_STATE, SSD_INNER), f32), pltpu.VMEM((SUBLANES + ll, SSD_CONV_CH), f32)],
        compiler_params=_params(("arbitrary", "arbitrary")),
        name="ssd_mix",
    )(z, u, fdt, conv_w, conv_b.reshape(1, -1), pad(dt_bias), pad(a_log),
      jnp.repeat(d_skip, SSD_HEAD_DIM).reshape(1, -1), gnorm.reshape(1, -1), jnp.asarray(e, dtype=bf16), tri, h0, c0)


def _hgrn_kernel(q_ref, f_ref, i_ref, g_ref, la_ref, lb_ref, om_ref, gn_ref, tri_ref, s0_ref,
                 o_ref, sout_ref, st_ref, *, ll, lv):
    c = pl.program_id(1)

    @pl.when(c == 0)
    def _():
        for h in range(HEADS):
            st_ref[h] = s0_ref[h].T

    def padded(ref):
        x = ref[...]
        return x if lv == ll else jnp.concatenate([x, jnp.zeros((ll - lv, WIDTH), f32)], axis=0)

    fl = padded(f_ref)
    a = la_ref[...]
    b = lb_ref[...] + _log_sigmoid(fl)
    lf = jnp.maximum(a, b) + jnp.log(1.0 + jnp.exp(-jnp.abs(a - b)))
    key = om_ref[...] * jax.nn.sigmoid(-fl)
    q = _silu(padded(q_ref))
    v = padded(i_ref)
    if lv < ll:
        live = lax.broadcasted_iota(jnp.int32, (ll, WIDTH), 0) < lv
        lf = jnp.where(live, lf, 0.0)
        key = jnp.where(live, key, 0.0)
    cum = jnp.dot(tri_ref[...], lf, preferred_element_type=f32, precision=HIGHEST)
    gate = _silu(padded(g_ref))

    nsub = ll // HG_SUB
    lrow = lax.broadcasted_iota(jnp.int32, (HG_SUB, HEAD_DIM), 0)
    lane = lax.broadcasted_iota(jnp.int32, (HG_SUB, HG_SUB), 1)
    lrow2 = lax.broadcasted_iota(jnp.int32, (HG_SUB, HG_SUB), 0)
    for h in range(HEADS):
        sl = slice(h * HEAD_DIM, (h + 1) * HEAD_DIM)
        qh, kh, vh, ch = q[:, sl], key[:, sl], v[:, sl], cum[:, sl]
        vb = vh.astype(bf16)
        st = st_ref[h]
        c_last = ch[ll - 1:ll]
        o_rows = []
        for blk in range(nsub):
            r0 = blk * HG_SUB
            cb_, qb_, kb_ = ch[r0:r0 + HG_SUB], qh[r0:r0 + HG_SUB], kh[r0:r0 + HG_SUB]
            diag = jnp.zeros((HG_SUB, HG_SUB), f32)
            for s in range(HG_SUB):
                ok = lrow >= s
                e = jnp.exp(jnp.where(ok, cb_ - cb_[s:s + 1], 0.0))
                val = jnp.sum(jnp.where(ok, qb_ * kb_[s:s + 1] * e, 0.0), axis=-1, keepdims=True)
                diag = jnp.where(lane == s, val, diag)
            diag = jnp.where(lane <= lrow2, diag, 0.0)
            parts = []
            if r0 > 0:
                ref_c = ch[r0 - 1:r0]
                qt = (qb_ * jnp.exp(cb_ - ref_c)).astype(bf16)
                kt = (kh[:r0] * jnp.exp(ref_c - ch[:r0])).astype(bf16)
                parts.append(_dot_nt(qt, kt))
            parts.append(diag)
            if r0 + HG_SUB < ll:
                parts.append(jnp.zeros((HG_SUB, ll - r0 - HG_SUB), f32))
            att = jnp.concatenate(parts, axis=1) if len(parts) > 1 else parts[0]
            o_rows.append(_dot(att.astype(bf16), vb))
        oh = jnp.concatenate(o_rows, axis=0) + _dot_nt((qh * jnp.exp(ch)).astype(bf16), st.astype(bf16))
        st_ref[h] = st * jnp.exp(c_last) + _dot_tn(vb, (kh * jnp.exp(c_last - ch)).astype(bf16))
        ms = jnp.mean(oh * oh, axis=-1, keepdims=True)
        on = oh * lax.rsqrt(ms + NORM_EPS) * gn_ref[:, sl] * gate[:, sl]
        o_ref[:, sl] = on[:lv].astype(o_ref.dtype)

    @pl.when(c == pl.num_programs(1) - 1)
    def _():
        for h in range(HEADS):
            sout_ref[h] = st_ref[h].T


def hgrn_mix(proj, lower_bound, gnorm, s0, *, lv):
    m = proj.shape[0]
    nseq = s0.shape[0]
    ll = HG_CHUNK
    assert lv <= ll and lv % SUBLANES == 0 and m % (nseq * lv) == 0
    nchunk = m // (nseq * lv)
    lb = lower_bound.reshape(1, WIDTH).astype(f32)
    r = np.arange(ll)
    tri = jnp.asarray((r[:, None] >= r[None, :]).astype(np.float32))
    col = lambda k: pl.BlockSpec((lv, WIDTH), lambda i, c, k=k: (i * nchunk + c, k))
    const = lambda i, c: (0, 0)
    seq4 = lambda i, c: (i, 0, 0, 0)
    return pl.pallas_call(
        functools.partial(_hgrn_kernel, ll=ll, lv=lv),
        out_shape=(jax.ShapeDtypeStruct((m, WIDTH), bf16),
                   jax.ShapeDtypeStruct((nseq, HEADS, HEAD_DIM, HEAD_DIM), f32)),
        grid=(nseq, nchunk),
        in_specs=[col(0), col(1), col(2), col(3),
                  pl.BlockSpec((1, WIDTH), const), pl.BlockSpec((1, WIDTH), const), pl.BlockSpec((1, WIDTH), const),
                  pl.BlockSpec((1, WIDTH), const), pl.BlockSpec((ll, ll), const),
                  pl.BlockSpec((None, HEADS, HEAD_DIM, HEAD_DIM), seq4)],
        out_specs=(pl.BlockSpec((lv, WIDTH), lambda i, c: (i * nchunk + c, 0)),
                   pl.BlockSpec((None, HEADS, HEAD_DIM, HEAD_DIM), seq4)),
        scratch_shapes=[pltpu.VMEM((HEADS, HEAD_DIM, HEAD_DIM), f32)],
        compiler_params=_params(("arbitrary", "arbitrary")),
        name="hgrn_mix",
    )(proj, proj, proj, proj, jnp.log(lb), jnp.log1p(-lb), 1.0 - lb, gnorm.reshape(1, WIDTH), tri, s0)


def _heads_to_rows(x, nb, n_new):
    return x.reshape(nb, n_new, HEADS, HEAD_DIM).transpose(0, 2, 1, 3).reshape(nb, HEADS * n_new, HEAD_DIM)


def _rows_to_heads(x, nb, n_new):
    return x.reshape(nb, HEADS, n_new, HEAD_DIM).transpose(0, 2, 1, 3).reshape(nb * n_new, WIDTH)


def _token_head_rows(x, nb, n_new):
    return x.reshape(nb, n_new * HEADS, HEAD_DIM)


def _trunk(x, mem_k, mem_v, p, lower_bounds, *, nseq, decode):
    m = x.shape[0]
    t = m // nseq
    assert decode or nseq == 1
    tm = min(m, 1024)
    depth = p["norm_ffn1"].shape[0]
    st = {}
    for layer in range(depth):
        i = layer // 2
        hmid = norm_matmul(x, p["norm_ffn1"][layer], [(p["ffn1_gate"], layer, 0), (p["ffn1_up"], layer, 0)],
                           n=D_FF, tn=512, tm=tm, out_dtype=bf16, swiglu=True, name="ffn1_up")
        x = matmul_residual(hmid, p["ffn1_down"], layer, x, scale=0.5, tm=tm, tn=512, tk=2816, name="ffn1_down")
        gain = p["norm_mix"][layer]
        if layer % 2 == 0:
            w_in = p["ab_w_in"][i]
            qkv = norm_matmul(x, gain, [(w_in, None, 0)], n=3 * WIDTH, tn=512, tm=tm, name="ab_qkv")
            off_z = 3 * WIDTH + HEADS
            off_x = off_z + SSD_INNER
            off_dt = off_x + SSD_CONV_CH
            w_fdt = jnp.concatenate([w_in[:, 3 * WIDTH:off_z], w_in[:, off_dt:off_dt + SSD_HEADS],
                                     jnp.zeros((D_MODEL, LANES - HEADS - SSD_HEADS), f32)], axis=1)
            zz = norm_matmul(x, gain, [(w_in[:, off_z:off_x], None, 0)], n=SSD_INNER, tn=512, tm=tm, name="ab_z")
            uu = norm_matmul(x, gain, [(w_in[:, off_x:off_dt], None, 0)], n=SSD_CONV_CH, tn=512, tm=tm, name="ab_xbc")
            fdt = norm_matmul(x, gain, [(w_fdt, None, 0)], n=LANES, tn=LANES, tm=tm, name="ab_fdt")
            logf, ck = fox_gate(fdt, p["ab_fox_fbias"][i], seg=t, tb=min(m, 512))
            k_new, v_new = qkv[:, WIDTH:2 * WIDTH], qkv[:, 2 * WIDTH:]
            if decode:
                bias_new = (-ck).T.reshape(nseq, 1, t * HEADS)
                o = decode_attention("fox", _heads_to_rows(qkv[:, :WIDTH], nseq, t), _token_head_rows(k_new, nseq, t),
                                     _token_head_rows(v_new, nseq, t), p["cache_fox_k"], p["cache_fox_v"],
                                     p["page_table"], bias_new=bias_new,
                                     bias_past=fox_past_bias(p["cache_fox_logf"], p["page_table"]))
                o_fox = _rows_to_heads(o, nseq, t).astype(bf16)
                h0, c0, lv = p["state_ssd"][i].reshape(nseq, SSD_INNER, SSD_STATE), p["state_ssd_conv"][i], t
            else:
                o_fox = fox_flash(qkv, ck, tq=1024, tk=1024, sub=512)
                h0 = jnp.zeros((nseq, SSD_INNER, SSD_STATE), f32)
                c0 = jnp.zeros((nseq, SSD_CONV - 1, SSD_CONV_CH), f32)
                lv = SSD_CHUNK
            y, h_new, c_new = ssd_mix(zz, uu, fdt, p["ab_conv_w"][i], p["ab_conv_b"][i], p["ab_dt_bias"][i],
                                      p["ab_A_log"][i], p["ab_D"][i], p["ab_ssd_norm"][i], h0, c0, lv=lv)
            mix_in = jnp.concatenate([o_fox, y], axis=1)
            x = matmul_residual(mix_in, p["ab_w_out"], i, x, scale=1.0, tm=tm, tn=1024, tk=1024, name="ab_out")
            st.setdefault("fox_k", []).append(k_new.reshape(nseq, t, HEADS, HEAD_DIM))
            st.setdefault("fox_v", []).append(v_new.reshape(nseq, t, HEADS, HEAD_DIM))
            st.setdefault("fox_logf", []).append(logf.reshape(nseq, t, HEADS))
            st.setdefault("ssd_conv", []).append(c_new)
            st.setdefault("ssd_state", []).append(h_new.reshape(nseq, SSD_HEADS, SSD_HEAD_DIM, SSD_STATE))
        else:
            proj = norm_matmul(x, gain, [(p["cd_w_in"], i, 0)], n=7 * WIDTH, tn=512, tm=tm, name="cd_in")
            k_new, v_new = proj[:, 5 * WIDTH:6 * WIDTH], proj[:, 6 * WIDTH:]
            if decode:
                s0, lv = p["state_hgrn"][i], t
                o = decode_attention("sb", _heads_to_rows(proj[:, 4 * WIDTH:5 * WIDTH], nseq, t),
                                     _token_head_rows(k_new, nseq, t), _token_head_rows(v_new, nseq, t),
                                     p["cache_sb_k"], p["cache_sb_v"], p["page_table"])
                o_sb = _rows_to_heads(o, nseq, t).astype(bf16)
            else:
                s0, lv = jnp.zeros((nseq, HEADS, HEAD_DIM, HEAD_DIM), f32), HG_CHUNK
                o_sb = sb_flash(proj, 4 * HEADS, tq=1024, tk=512, sub=1024)
            o_hg, s_new = hgrn_mix(proj, lower_bounds[layer], p["cd_hg_norm"][i], s0, lv=lv)
            mix_in = jnp.concatenate([o_hg, o_sb], axis=1)
            x = matmul_residual(mix_in, p["cd_w_out"], i, x, scale=1.0, tm=tm, tn=1024, tk=1024, name="cd_out")
            st.setdefault("hgrn", []).append(s_new)
            st.setdefault("sb_k", []).append(k_new.reshape(nseq, t, HEADS, HEAD_DIM))
            st.setdefault("sb_v", []).append(v_new.reshape(nseq, t, HEADS, HEAD_DIM))
        qm = norm_matmul(x, p["norm_mem"][layer], [(p["mem_wq"], layer, 0)], n=MEM_WIDTH, tn=MEM_WIDTH, tm=tm,
                         name="mem_q")
        om = mem_attention(qm, mem_k, mem_v, layer, tq=min(t, 1024))
        x = matmul_residual(om, p["mem_wo"], layer, x, scale=1.0, tm=tm, tn=1024, tk=MEM_WIDTH, name="mem_out")
        hmid = norm_matmul(x, p["norm_ffn2"][layer], [(p["ffn2_gate"], layer, 0), (p["ffn2_up"], layer, 0)],
                           n=D_FF, tn=512, tm=tm, out_dtype=bf16, swiglu=True, name="ffn2_up")
        x = matmul_residual(hmid, p["ffn2_down"], layer, x, scale=0.5, tm=tm, tn=512, tk=2816, name="ffn2_down")
    y = rmsnorm_rows(x, p["norm_final"], tm=min(m, 512))
    return y, {k: jnp.stack(v) for k, v in st.items()}


def kernel(x_prompt, x_sample, cache_fox_k, cache_fox_v, cache_fox_logf, state_ssd_conv, state_ssd, state_hgrn, cache_sb_k, cache_sb_v, cache_mem_k, cache_mem_v, page_table, mem_prompt, norm_ffn1, ffn1_gate, ffn1_up, ffn1_down, norm_mix, ab_w_in, ab_fox_fbias, ab_conv_w, ab_conv_b, ab_dt_bias, ab_A_log, ab_D, ab_ssd_norm, ab_w_out, cd_w_in, hg_lower_bound, cd_hg_norm, cd_w_out, norm_mem, norm_memkv, mem_wq, mem_wk, mem_wv, mem_wo, norm_ffn2, ffn2_gate, ffn2_up, ffn2_down, norm_final):
    p = dict(norm_ffn1=norm_ffn1, ffn1_gate=ffn1_gate, ffn1_up=ffn1_up, ffn1_down=ffn1_down, norm_mix=norm_mix,
             ab_w_in=ab_w_in, ab_fox_fbias=ab_fox_fbias, ab_conv_w=ab_conv_w, ab_conv_b=ab_conv_b,
             ab_dt_bias=ab_dt_bias, ab_A_log=ab_A_log, ab_D=ab_D, ab_ssd_norm=ab_ssd_norm, ab_w_out=ab_w_out,
             cd_w_in=cd_w_in, cd_hg_norm=cd_hg_norm, cd_w_out=cd_w_out, norm_mem=norm_mem, mem_wq=mem_wq,
             mem_wo=mem_wo, norm_ffn2=norm_ffn2, ffn2_gate=ffn2_gate, ffn2_up=ffn2_up, ffn2_down=ffn2_down,
             norm_final=norm_final, cache_fox_k=cache_fox_k, cache_fox_v=cache_fox_v, cache_fox_logf=cache_fox_logf,
             cache_sb_k=cache_sb_k, cache_sb_v=cache_sb_v, page_table=page_table, state_ssd=state_ssd,
             state_ssd_conv=state_ssd_conv, state_hgrn=state_hgrn)
    depth = norm_ffn1.shape[0]
    probs = jax.nn.softmax(hg_lower_bound.astype(f32), axis=0)
    lower_bounds = jnp.cumsum(probs, axis=0) - probs[0]

    bp, seq, _ = x_prompt.shape
    bs, dseq, _ = x_sample.shape
    mem_len = mem_prompt.shape[1]
    mem_rows = mem_prompt.reshape(bp * mem_len, D_MODEL)
    mk = [norm_matmul(mem_rows, norm_memkv[l], [(mem_wk, l, 0)], n=MEM_WIDTH, tn=MEM_WIDTH, tm=mem_len, name="mem_k")
          for l in range(depth)]
    mv = [norm_matmul(mem_rows, norm_memkv[l], [(mem_wv, l, 0)], n=MEM_WIDTH, tn=MEM_WIDTH, tm=mem_len, name="mem_v")
          for l in range(depth)]
    mem_k_p = jnp.stack(mk).reshape(depth, bp, mem_len, MEM_WIDTH)
    mem_v_p = jnp.stack(mv).reshape(depth, bp, mem_len, MEM_WIDTH)

    y_p, sp = _trunk(x_prompt.reshape(bp * seq, D_MODEL), mem_k_p, mem_v_p, p, lower_bounds, nseq=bp, decode=False)
    y_s, ss = _trunk(x_sample.reshape(bs * dseq, D_MODEL), cache_mem_k.reshape(depth, bs, mem_len, MEM_WIDTH),
                     cache_mem_v.reshape(depth, bs, mem_len, MEM_WIDTH), p, lower_bounds, nseq=bs, decode=True)

    m5 = (depth, bp, mem_len, MEM_HEADS, HEAD_DIM)
    return (y_p.reshape(bp, seq, D_MODEL), y_s.reshape(bs, dseq, D_MODEL),
            sp["fox_k"], sp["fox_v"], sp["fox_logf"], sp["ssd_conv"], sp["ssd_state"], sp["hgrn"], sp["sb_k"], sp["sb_v"],
            mem_k_p.reshape(m5), mem_v_p.reshape(m5),
            ss["fox_k"], ss["fox_v"], ss["fox_logf"], ss["ssd_conv"], ss["ssd_state"], ss["hgrn"], ss["sb_k"], ss["sb_v"])
```

```python
import functools
import math

import numpy as np
import jax
import jax.numpy as jnp
from jax import lax
from jax.experimental import pallas as pl
from jax.experimental.pallas import tpu as pltpu

f32 = jnp.float32
bf16 = jnp.bfloat16
HIGHEST = lax.Precision.HIGHEST

D_MODEL = 2048
D_FF = 5632
HEADS = 8
HEAD_DIM = 128
WIDTH = HEADS * HEAD_DIM
SSD_HEADS = 32
SSD_HEAD_DIM = 64
SSD_INNER = SSD_HEADS * SSD_HEAD_DIM
SSD_GROUPS = 4
SSD_STATE = 128
SSD_CONV = 4
SSD_CONV_CH = SSD_INNER + 2 * SSD_GROUPS * SSD_STATE
SSD_CHUNK = 128
HG_CHUNK = 64
HG_SUB = 16
MEM_HEADS = 4
MEM_WIDTH = MEM_HEADS * HEAD_DIM
PAGE = 128
NORM_EPS = 1e-6
NEG_INF = -1e30
ATTN_SCALE = HEAD_DIM ** -0.5
LOG2E = math.log2(math.e)

LANES = 128
SUBLANES = 8
VMEM_LIMIT_BYTES = 56 * 1024 * 1024
FDT_LANE0 = 8

NT_DIMS = (((1,), (1,)), ((), ()))
TN_DIMS = (((0,), (0,)), ((), ()))


def _params(sem):
    return pltpu.CompilerParams(dimension_semantics=sem, vmem_limit_bytes=VMEM_LIMIT_BYTES)


def _dot(a, b):
    return jnp.dot(a, b, preferred_element_type=f32)


def _dot_nt(a, b):
    return lax.dot_general(a, b, NT_DIMS, preferred_element_type=f32)


def _dot_tn(a, b):
    return lax.dot_general(a, b, TN_DIMS, preferred_element_type=f32)


def _split_dot(x, e):
    hi = x.astype(bf16)
    lo = (x - hi.astype(f32)).astype(bf16)
    return _dot(jnp.concatenate([hi, lo], axis=1), jnp.concatenate([e, e], axis=0))


def _silu(x):
    return x * jax.nn.sigmoid(x)


def _log_sigmoid(x):
    return jnp.minimum(x, 0.0) - jnp.log(1.0 + jnp.exp(-jnp.abs(x)))


def _nmm_kernel(x_ref, g_ref, *refs, n_w, swiglu):
    w_refs, o_ref, xn_ref = refs[:n_w], refs[n_w], refs[n_w + 1]

    @pl.when(pl.program_id(1) == 0)
    def _():
        x = x_ref[...]
        ms = jnp.mean(x * x, axis=-1, keepdims=True)
        xn_ref[...] = (x * lax.rsqrt(ms + NORM_EPS) * g_ref[...]).astype(bf16)

    xn = xn_ref[...]
    if swiglu:
        g = _dot(xn, w_refs[0][...].astype(bf16))
        u = _dot(xn, w_refs[1][...].astype(bf16))
        o_ref[...] = (_silu(g) * u).astype(o_ref.dtype)
    else:
        o_ref[...] = _dot(xn, w_refs[0][...].astype(bf16)).astype(o_ref.dtype)


def _w_spec(w, lead, k, tn, off):
    if w.ndim == 3:
        return pl.BlockSpec((None, k, tn), lambda i, j: (lead, 0, j + off))
    return pl.BlockSpec((k, tn), lambda i, j: (0, j + off))


def norm_matmul(x, gain, ws, *, n, tn, tm, out_dtype=f32, swiglu=False, name="nmm"):
    m, k = x.shape
    assert m % tm == 0 and n % tn == 0
    in_specs = [pl.BlockSpec((tm, k), lambda i, j: (i, 0)), pl.BlockSpec((1, k), lambda i, j: (0, 0))]
    in_specs += [_w_spec(w, lead, k, tn, off) for (w, lead, off) in ws]
    return pl.pallas_call(
        functools.partial(_nmm_kernel, n_w=len(ws), swiglu=swiglu),
        out_shape=jax.ShapeDtypeStruct((m, n), out_dtype),
        grid=(m // tm, n // tn),
        in_specs=in_specs,
        out_specs=pl.BlockSpec((tm, tn), lambda i, j: (i, j)),
        scratch_shapes=[pltpu.VMEM((tm, k), bf16)],
        compiler_params=_params(("arbitrary", "arbitrary")),
        name=name,
    )(x, gain.reshape(1, k), *[w for (w, _, _) in ws])


def _mmr_kernel(*refs, widths, scale):
    a_refs = refs[:len(widths)]
    w_ref, r_ref, o_ref = refs[len(widths):]
    w = w_ref[...].astype(bf16)
    p, off = None, 0
    for a_ref, kw in zip(a_refs, widths):
        d = _dot(a_ref[...].astype(bf16), w[off:off + kw])
        p = d if p is None else p + d
        off += kw
    o_ref[...] = r_ref[...] + scale * p


def matmul_residual(a_list, w, lead, res, *, scale, tm, tn, name="mmr"):
    m = a_list[0].shape[0]
    widths = tuple(a.shape[1] for a in a_list)
    kk = sum(widths)
    n = res.shape[1]
    assert m % tm == 0 and n % tn == 0 and w.shape[-2] == kk
    if w.ndim == 3:
        w_spec = pl.BlockSpec((None, kk, tn), lambda i, j: (lead, 0, j))
    else:
        w_spec = pl.BlockSpec((kk, tn), lambda i, j: (0, j))
    return pl.pallas_call(
        functools.partial(_mmr_kernel, widths=widths, scale=scale),
        out_shape=jax.ShapeDtypeStruct((m, n), f32),
        grid=(m // tm, n // tn),
        in_specs=[pl.BlockSpec((tm, kw), lambda i, j: (i, 0)) for kw in widths] + [
            w_spec, pl.BlockSpec((tm, tn), lambda i, j: (i, j))],
        out_specs=pl.BlockSpec((tm, tn), lambda i, j: (i, j)),
        compiler_params=_params(("arbitrary", "arbitrary")),
        name=name,
    )(*a_list, w, res)


def _rmsnorm_kernel(x_ref, g_ref, o_ref):
    x = x_ref[...]
    ms = jnp.mean(x * x, axis=-1, keepdims=True)
    o_ref[...] = x * lax.rsqrt(ms + NORM_EPS) * g_ref[...]


def rmsnorm_rows(x, gain, *, tm):
    m, k = x.shape
    return pl.pallas_call(
        _rmsnorm_kernel,
        out_shape=jax.ShapeDtypeStruct((m, k), f32),
        grid=(m // tm,),
        in_specs=[pl.BlockSpec((tm, k), lambda i: (i, 0)), pl.BlockSpec((1, k), lambda i: (0, 0))],
        out_specs=pl.BlockSpec((tm, k), lambda i: (i, 0)),
        compiler_params=_params(("arbitrary",)),
        name="final_norm",
    )(x, gain.reshape(1, k))


def _fox_gate_kernel(fdt_ref, bias_ref, u_ref, lf_ref, ck_ref, carry_ref, *, tb, seg, carry):
    lf = _log_sigmoid(fdt_ref[...] + bias_ref[...])
    lf_ref[...] = lf[:, :HEADS]
    lft = lf.T[:HEADS, :]
    cum = jnp.dot(lft, u_ref[...], preferred_element_type=f32, precision=HIGHEST)
    if carry:
        @pl.when((pl.program_id(0) * tb) % seg == 0)
        def _():
            carry_ref[...] = jnp.zeros_like(carry_ref)
        cum = cum + carry_ref[:, 0:1]
        carry_ref[...] = jnp.broadcast_to(cum[:, tb - 1:tb], carry_ref.shape)
    ck_ref[...] = cum


def fox_gate(fdt, fbias, *, seg, tb):
    m = fdt.shape[0]
    assert m % tb == 0 and (seg % tb == 0 or tb % seg == 0)
    r = np.arange(tb)
    u = ((r[:, None] <= r[None, :]) & (r[:, None] // seg == r[None, :] // seg)).astype(np.float32)
    bias = jnp.zeros((1, LANES), f32).at[0, :HEADS].set(fbias)
    return pl.pallas_call(
        functools.partial(_fox_gate_kernel, tb=tb, seg=seg, carry=seg > tb),
        out_shape=(jax.ShapeDtypeStruct((m, HEADS), f32), jax.ShapeDtypeStruct((HEADS, m), f32)),
        grid=(m // tb,),
        in_specs=[pl.BlockSpec((tb, LANES), lambda i: (i, 0)), pl.BlockSpec((1, LANES), lambda i: (0, 0)),
                  pl.BlockSpec((tb, tb), lambda i: (0, 0))],
        out_specs=(pl.BlockSpec((tb, HEADS), lambda i: (i, 0)), pl.BlockSpec((HEADS, tb), lambda i: (0, i))),
        scratch_shapes=[pltpu.VMEM((HEADS, LANES), f32)],
        compiler_params=_params(("arbitrary",)),
        name="fox_gate",
    )(fdt, bias, jnp.asarray(u))


def _lanes(x, width):
    if width <= LANES:
        return x[:, :width]
    return jnp.concatenate([x] * (width // LANES), axis=1)


def _softmax_step(s, vb, state):
    m_prev, l_prev, acc = state
    m_new = jnp.maximum(m_prev, jnp.max(s, axis=-1, keepdims=True))
    alpha = jnp.exp2(m_prev - m_new)
    p = jnp.exp2(s - _lanes(m_new, s.shape[1]))
    l_new = alpha * l_prev + jnp.sum(p, axis=-1, keepdims=True)
    return m_new, l_new, alpha * acc + _dot(p.astype(bf16), vb)


def _softmax_update(s, vb, m_ref, l_ref, acc_ref, rows):
    m_ref[rows, :], l_ref[rows, :], acc_ref[rows, :] = _softmax_step(
        s, vb, (m_ref[rows, :], l_ref[rows, :], acc_ref[rows, :]))


def _fox_flash_kernel(q_ref, k_ref, v_ref, ck_ref, o_ref, qs_ref, m_ref, l_ref, acc_ref, *, tq, tk, sub):
    i = pl.program_id(1)
    nkb = tq // tk
    qs_ref[...] = (q_ref[...] * (ATTN_SCALE * LOG2E)).astype(bf16)
    m_ref[...] = jnp.full(m_ref.shape, NEG_INF, f32)
    l_ref[...] = jnp.zeros(l_ref.shape, f32)
    acc_ref[...] = jnp.zeros(acc_ref.shape, f32)

    def step(j, diag):
        start = pl.multiple_of(j * tk, tk)
        kb = k_ref[pl.ds(start, tk), :].astype(bf16)
        vb = v_ref[pl.ds(start, tk), :].astype(bf16)
        bias = ck_ref[:, pl.ds(start, tk)]
        for r0 in range(0, tq, sub):
            masked = False
            if diag is not None:
                if diag * tk > r0 + sub - 1:
                    continue
                masked = diag * tk + tk - 1 > r0
            rows = slice(r0, r0 + sub)
            s = _dot_nt(qs_ref[rows, :], kb) - bias
            if masked:
                row = lax.broadcasted_iota(jnp.int32, (sub, tk), 0) + r0
                col = lax.broadcasted_iota(jnp.int32, (sub, tk), 1) + diag * tk
                s = jnp.where(col <= row, s, NEG_INF)
            _softmax_update(s, vb, m_ref, l_ref, acc_ref, rows)

    def body(j, c):
        step(j, None)
        return c

    lax.fori_loop(0, i * nkb, body, 0)
    for d in range(nkb):
        step(i * nkb + d, d)
    o_ref[...] = (acc_ref[...] / l_ref[...]).astype(o_ref.dtype)


def fox_flash(qkv, ck, *, tq, tk, sub):
    t = qkv.shape[0]
    assert t % tq == 0 and tq % tk == 0 and tq % sub == 0 and tk % LANES == 0
    stat = pltpu.VMEM((tq, LANES), f32)
    return pl.pallas_call(
        functools.partial(_fox_flash_kernel, tq=tq, tk=tk, sub=sub),
        out_shape=jax.ShapeDtypeStruct((t, WIDTH), bf16),
        grid=(HEADS, t // tq),
        in_specs=[pl.BlockSpec((tq, HEAD_DIM), lambda h, i: (i, h)),
                  pl.BlockSpec((t, HEAD_DIM), lambda h, i: (0, HEADS + h)),
                  pl.BlockSpec((t, HEAD_DIM), lambda h, i: (0, 2 * HEADS + h)),
                  pl.BlockSpec((None, 1, t), lambda h, i: (h, 0, 0))],
        out_specs=pl.BlockSpec((tq, HEAD_DIM), lambda h, i: (i, h)),
        scratch_shapes=[pltpu.VMEM((tq, HEAD_DIM), bf16), stat, stat, pltpu.VMEM((tq, HEAD_DIM), f32)],
        compiler_params=_params(("arbitrary", "arbitrary")),
        name="fox_flash",
    )(qkv, qkv, qkv, (ck * LOG2E).reshape(HEADS, 1, t))


def _sb_tile(z, valid, tri_ext, r):
    nr, sk = z.shape
    cw = tri_ext.shape[0]
    lk = -(jnp.maximum(z, 0.0) + jnp.log2(1.0 + jnp.exp2(-jnp.abs(z))))
    if valid is not None:
        lk = jnp.where(valid, lk, 0.0)
    nch = sk // cw
    stacked = lk if nch == 1 else jnp.concatenate([lk[:, c * cw:(c + 1) * cw] for c in range(nch)], axis=0)
    inc = _split_dot(stacked, tri_ext)
    sums = [None] * nch
    for c in range(nch - 1, -1, -1):
        blk = inc[c * nr:(c + 1) * nr]
        sums[c] = blk[:, :cw] + _lanes(r, cw)
        r = r + blk[:, cw:]
    w = jnp.exp2(z + (sums[0] if nch == 1 else jnp.concatenate(sums, axis=1)))
    if valid is not None:
        w = jnp.where(valid, w, 0.0)
    return w, r


def _sb_flash_kernel(q_ref, k_ref, v_ref, tri_ref, o_ref, qs_ref, r_ref, acc_ref, *, tq, tk, sub):
    i = pl.program_id(1)
    nkb = tq // tk
    qs_ref[...] = (q_ref[...] * (ATTN_SCALE * LOG2E)).astype(bf16)
    r_ref[...] = jnp.zeros(r_ref.shape, f32)
    acc_ref[...] = jnp.zeros(acc_ref.shape, f32)

    def step(j, diag):
        start = pl.multiple_of(j * tk, tk)
        kb = k_ref[pl.ds(start, tk), :].astype(bf16)
        vb = v_ref[pl.ds(start, tk), :].astype(bf16)
        for r0 in range(0, tq, sub):
            valid = None
            if diag is not None:
                if diag * tk >= r0 + sub - 1:
                    continue
                if diag * tk + tk - 1 >= r0:
                    row = lax.broadcasted_iota(jnp.int32, (sub, tk), 0) + r0
                    col = lax.broadcasted_iota(jnp.int32, (sub, tk), 1) + diag * tk
                    valid = col < row
            rows = slice(r0, r0 + sub)
            w, r_new = _sb_tile(_dot_nt(qs_ref[rows, :], kb), valid, tri_ref[...], r_ref[rows, :])
            acc_ref[rows, :] += _dot(w.astype(bf16), vb)
            r_ref[rows, :] = r_new

    for d in range(nkb - 1, -1, -1):
        step(i * nkb + d, d)

    def body(t, c):
        step(i * nkb - 1 - t, None)
        return c

    lax.fori_loop(0, i * nkb, body, 0)
    o_ref[...] = acc_ref[...].astype(o_ref.dtype)


def _tri_ext(n):
    r = np.arange(n)
    tri = (r[:, None] >= r[None, :]).astype(np.float32)
    return jnp.asarray(np.concatenate([tri, np.ones((n, LANES), np.float32)], axis=1), dtype=bf16)


def sb_flash(proj, col0, *, tq, tk, sub):
    t = proj.shape[0]
    assert t % tq == 0 and tq % tk == 0 and tq % sub == 0 and tk % LANES == 0
    return pl.pallas_call(
        functools.partial(_sb_flash_kernel, tq=tq, tk=tk, sub=sub),
        out_shape=jax.ShapeDtypeStruct((t, WIDTH), bf16),
        grid=(HEADS, t // tq),
        in_specs=[pl.BlockSpec((tq, HEAD_DIM), lambda h, i: (i, col0 + h)),
                  pl.BlockSpec((t, HEAD_DIM), lambda h, i: (0, col0 + HEADS + h)),
                  pl.BlockSpec((t, HEAD_DIM), lambda h, i: (0, col0 + 2 * HEADS + h)),
                  pl.BlockSpec((LANES, 2 * LANES), lambda h, i: (0, 0))],
        out_specs=pl.BlockSpec((tq, HEAD_DIM), lambda h, i: (i, h)),
        scratch_shapes=[pltpu.VMEM((tq, HEAD_DIM), bf16), pltpu.VMEM((tq, LANES), f32),
                        pltpu.VMEM((tq, HEAD_DIM), f32)],
        compiler_params=_params(("arbitrary", "arbitrary")),
        name="sb_flash",
    )(proj, proj, proj, _tri_ext(LANES))


PAGE_TILE_ROWS = PAGE * HEADS // LANES
TOK_PER_ROW = LANES // HEADS


def _past_bias_kernel(pt_ref, lf_hbm, dm_ref, hm_ref, up_ref, o_ref, buf, sem, *, n_pages):
    b = pl.program_id(0)

    def copy(j):
        return pltpu.make_async_copy(lf_hbm.at[pt_ref[b, j]], buf.at[j], sem)

    def start(j, c):
        copy(j).start()
        return c

    def wait(j, c):
        copy(j).wait()
        return c

    lax.fori_loop(0, n_pages, start, 0)
    lax.fori_loop(0, n_pages, wait, 0)

    def hdot(a, m_ref):
        return jnp.dot(a, m_ref[...], preferred_element_type=f32, precision=HIGHEST)

    ys = [buf[:, r, :] for r in range(PAGE_TILE_ROWS)]
    later = [None] * PAGE_TILE_ROWS
    acc = jnp.zeros_like(ys[0])
    for r in range(PAGE_TILE_ROWS - 1, -1, -1):
        later[r] = acc
        acc = acc + ys[r]
    pages_after = hdot(jnp.dot(up_ref[...], acc, preferred_element_type=f32, precision=HIGHEST), hm_ref)
    for r in range(PAGE_TILE_ROWS):
        o_ref[:, r, :] = hdot(ys[r], dm_ref) + hdot(later[r], hm_ref) + pages_after


def fox_past_bias(cache_logf, page_table):
    nb, n_pages = page_table.shape
    n_pool = cache_logf.shape[1]
    lane = np.arange(LANES)
    same_head = lane[:, None] % HEADS == lane[None, :] % HEADS
    dm = (same_head & (lane[:, None] // HEADS > lane[None, :] // HEADS)).astype(np.float32)
    hm = same_head.astype(np.float32)
    pg = np.arange(n_pages)
    up = (pg[None, :] > pg[:, None]).astype(np.float32)
    const = lambda shape: pl.BlockSpec(shape, lambda b, pt: (0, 0))
    return pl.pallas_call(
        functools.partial(_past_bias_kernel, n_pages=n_pages),
        out_shape=jax.ShapeDtypeStruct((nb, n_pages, PAGE_TILE_ROWS, LANES), f32),
        grid_spec=pltpu.PrefetchScalarGridSpec(
            num_scalar_prefetch=1,
            grid=(nb,),
            in_specs=[pl.BlockSpec(memory_space=pl.ANY), const((LANES, LANES)), const((LANES, LANES)),
                      const((n_pages, n_pages))],
            out_specs=pl.BlockSpec((None, n_pages, PAGE_TILE_ROWS, LANES), lambda b, pt: (b, 0, 0, 0)),
            scratch_shapes=[pltpu.VMEM((n_pages, PAGE_TILE_ROWS, LANES), f32), pltpu.SemaphoreType.DMA(())],
        ),
        compiler_params=_params(("arbitrary",)),
        name="fox_past_bias",
    )(page_table, cache_logf.reshape(n_pool, PAGE_TILE_ROWS, LANES), jnp.asarray(dm), jnp.asarray(hm), jnp.asarray(up))


def _decode_kernel(pt_ref, q_ref, kn_ref, vn_ref, hm_ref, *refs, mode, pp, n_new):
    k_refs, v_refs = refs[:pp], refs[pp:2 * pp]
    refs = refs[2 * pp:]
    if mode == "fox":
        bn_ref, ap_refs = refs[0], refs[1:1 + pp]
        o_ref = refs[1 + pp]
        state_refs = refs[2 + pp:]
    else:
        tri_ref = refs[0]
        o_ref = refs[1]
        state_refs = refs[2:]
    j = pl.program_id(1)
    rows = HEADS * n_new
    page_cols = PAGE * HEADS
    qs = (q_ref[...] * (ATTN_SCALE * LOG2E)).astype(bf16)

    def own_head(cols):
        rh = lax.broadcasted_iota(jnp.int32, (rows, cols), 0) // n_new
        ch = lax.broadcasted_iota(jnp.int32, (rows, cols), 1) % HEADS
        return rh == ch

    def fox_block(kbs, vbs, biases, valid, state):
        m_prev, l_prev, acc = state
        ss = [jnp.where(valid, _dot_nt(qs, kb) + bias, NEG_INF) for kb, bias in zip(kbs, biases)]
        m_new = m_prev
        for s in ss:
            m_new = jnp.maximum(m_new, jnp.max(s, axis=-1, keepdims=True))
        alpha = jnp.exp2(m_prev - m_new)
        l_new, acc = alpha * l_prev, alpha * acc
        for s, vb in zip(ss, vbs):
            p = jnp.exp2(s - _lanes(m_new, s.shape[1]))
            l_new = l_new + jnp.sum(p, axis=-1, keepdims=True)
            acc = acc + _dot(p.astype(bf16), vb)
        return m_new, l_new, acc

    def sb_block(kbs, vbs, valid, state):
        r, acc = state
        cols = kbs[0].shape[0]
        cw = min(cols, LANES)
        tri = tri_ref[...] if cw == LANES else jnp.concatenate([tri_ref[:cw, :cw], tri_ref[:cw, LANES:]], axis=1)
        n = len(kbs)
        z = jnp.concatenate([_dot_nt(qs, kb) for kb in kbs], axis=1) if n > 1 else _dot_nt(qs, kbs[0])
        w, r = _sb_tile(z, jnp.concatenate([valid] * n, axis=1) if n > 1 else valid, tri, r)
        for p, vb in enumerate(vbs):
            acc = acc + _dot(w[:, p * cols:(p + 1) * cols].astype(bf16), vb)
        return r, acc

    @pl.when(j == 0)
    def _():
        cols = n_new * HEADS
        tq = lax.broadcasted_iota(jnp.int32, (rows, cols), 0) % n_new
        sk = lax.broadcasted_iota(jnp.int32, (rows, cols), 1) // HEADS
        kb, vb = kn_ref[...].astype(bf16), vn_ref[...].astype(bf16)
        zeros = jnp.zeros((rows, LANES), f32)
        if mode == "fox":
            state = fox_block([kb], [vb], [bn_ref[...] * LOG2E], own_head(cols) & (sk <= tq),
                              (jnp.full((rows, LANES), NEG_INF, f32), zeros, zeros))
        else:
            state = sb_block([kb], [vb], own_head(cols) & (sk < tq), (zeros, zeros))
        for ref, val in zip(state_refs, state):
            ref[...] = val

    kbs = [k_refs[p][...].reshape(page_cols, HEAD_DIM).astype(bf16) for p in range(pp)]
    vbs = [v_refs[p][...].reshape(page_cols, HEAD_DIM).astype(bf16) for p in range(pp)]
    state = tuple(ref[...] for ref in state_refs)
    if mode == "fox":
        biases = []
        for p in range(pp):
            ap = ap_refs[p][...] * LOG2E
            biases.append(jnp.concatenate(
                [jnp.broadcast_to(ap[r:r + 1], (rows, LANES)) for r in range(PAGE_TILE_ROWS)], axis=1))
        state = fox_block(kbs, vbs, biases, hm_ref[...] > 0.5, state)
    else:
        state = sb_block(kbs, vbs, hm_ref[...] > 0.5, state)
    for ref, val in zip(state_refs, state):
        ref[...] = val

    @pl.when(j == pl.num_programs(1) - 1)
    def _():
        if mode == "fox":
            o_ref[...] = state[2] / state[1]
        else:
            o_ref[...] = state[1]


def decode_attention(mode, q, k_new, v_new, cache_k, cache_v, page_table, *, bias_new=None, bias_past=None, pp=8):
    nb, rows, _ = q.shape
    n_new = rows // HEADS
    n_pages = page_table.shape[1]
    assert n_pages % pp == 0
    nsteps = n_pages // pp

    def page_of(b, j, pt, p):
        return pt[b, (nsteps - 1 - j) * pp + p]

    kv_specs = [pl.BlockSpec((None, None, PAGE, HEADS, HEAD_DIM),
                             lambda b, j, pt, p=p: (0, page_of(b, j, pt, p), 0, 0, 0)) for p in range(pp)]
    new_spec = pl.BlockSpec((None, rows, HEAD_DIM), lambda b, j, pt: (b, 0, 0))
    own_head = (np.arange(rows)[:, None] // n_new == np.arange(PAGE * HEADS)[None, :] % HEADS).astype(np.float32)
    in_specs = [new_spec, new_spec, new_spec, pl.BlockSpec((rows, PAGE * HEADS), lambda b, j, pt: (0, 0))]
    in_specs += kv_specs + kv_specs
    args = [q, k_new, v_new, jnp.asarray(own_head)] + [cache_k] * pp + [cache_v] * pp
    stat = pltpu.VMEM((rows, LANES), f32)
    if mode == "fox":
        in_specs.append(pl.BlockSpec((None, 1, rows), lambda b, j, pt: (b, 0, 0)))
        in_specs += [pl.BlockSpec((None, None, PAGE_TILE_ROWS, LANES),
                                  lambda b, j, pt, p=p: (b, (nsteps - 1 - j) * pp + p, 0, 0)) for p in range(pp)]
        args += [bias_new] + [bias_past] * pp
        scratch = [stat, stat, pltpu.VMEM((rows, HEAD_DIM), f32)]
    else:
        in_specs.append(pl.BlockSpec((LANES, 2 * LANES), lambda b, j, pt: (0, 0)))
        args.append(_tri_ext(LANES))
        scratch = [stat, pltpu.VMEM((rows, HEAD_DIM), f32)]
    return pl.pallas_call(
        functools.partial(_decode_kernel, mode=mode, pp=pp, n_new=n_new),
        out_shape=jax.ShapeDtypeStruct((nb, rows, HEAD_DIM), f32),
        grid_spec=pltpu.PrefetchScalarGridSpec(
            num_scalar_prefetch=1,
            grid=(nb, nsteps),
            in_specs=in_specs,
            out_specs=pl.BlockSpec((None, rows, HEAD_DIM), lambda b, j, pt: (b, 0, 0)),
            scratch_shapes=scratch,
        ),
        compiler_params=_params(("arbitrary", "arbitrary")),
        name=mode + "_decode",
    )(page_table, *args)


def _mem_attn_kernel(q_ref, k_ref, v_ref, o_ref):
    s = _dot_nt(q_ref[...].astype(bf16), k_ref[...].astype(bf16)) * ATTN_SCALE
    m = jnp.max(s, axis=-1, keepdims=True)
    p = jnp.exp(s - m)
    l = jnp.sum(p, axis=-1, keepdims=True)
    o_ref[...] = (_dot(p.astype(bf16), v_ref[...].astype(bf16)) / l).astype(o_ref.dtype)


def mem_attention(q, mem_k, mem_v, layer, *, tq):
    m = q.shape[0]
    nb, mem_len = mem_k.shape[1], mem_k.shape[2]
    per_b = m // nb
    assert per_b % tq == 0
    nq = per_b // tq
    kv_spec = pl.BlockSpec((None, None, mem_len, HEAD_DIM), lambda i, h: (layer, i // nq, 0, h))
    return pl.pallas_call(
        _mem_attn_kernel,
        out_shape=jax.ShapeDtypeStruct((m, MEM_WIDTH), bf16),
        grid=(m // tq, MEM_HEADS),
        in_specs=[pl.BlockSpec((tq, HEAD_DIM), lambda i, h: (i, h)), kv_spec, kv_spec],
        out_specs=pl.BlockSpec((tq, HEAD_DIM), lambda i, h: (i, h)),
        compiler_params=_params(("arbitrary", "arbitrary")),
        name="mem_attn",
    )(q, mem_k, mem_v)


def _ssd_kernel(z_ref, u_ref, fdt_ref, cw_ref, cb_ref, dtb_ref, alog_ref, dexp_ref, gn_ref, e_ref, tri_ref,
                h0_ref, c0_ref, y_ref, hout_ref, cout_ref, ht_ref, cbuf_ref, *, ll, lv):
    c = pl.program_id(1)
    npad = SSD_CONV - 1
    base = SUBLANES
    hpg = SSD_HEADS // SSD_GROUPS
    gw = hpg * SSD_HEAD_DIM

    @pl.when(c == 0)
    def _():
        ht_ref[...] = h0_ref[...].T
        cbuf_ref[base - npad:base, :] = c0_ref[...]
        if lv < ll:
            cbuf_ref[base:base + ll, :] = jnp.zeros((ll, SSD_CONV_CH), f32)

    cbuf_ref[base:base + lv, :] = u_ref[...]
    xbc = cb_ref[...] + sum(cw_ref[i:i + 1, :] * cbuf_ref[base - npad + i:base - npad + i + ll, :]
                            for i in range(SSD_CONV))
    conv_tail = cbuf_ref[base + lv - npad:base + lv, :]
    cbuf_ref[base - npad:base, :] = conv_tail
    xbc = _silu(xbc)

    fdt = fdt_ref[...] if lv == ll else jnp.concatenate([fdt_ref[...], jnp.zeros((ll - lv, LANES), f32)], axis=0)
    dt = jax.nn.softplus(fdt + dtb_ref[...])
    lane = lax.broadcasted_iota(jnp.int32, (ll, LANES), 1)
    row = lax.broadcasted_iota(jnp.int32, (ll, LANES), 0)
    live = (lane >= FDT_LANE0) & (lane < FDT_LANE0 + SSD_HEADS) & (row < lv)
    dt = jnp.where(live, dt, 0.0)
    xs = xbc[:, :SSD_INNER]
    if lv < ll:
        xs = jnp.where(lax.broadcasted_iota(jnp.int32, (ll, SSD_INNER), 0) < lv, xs, 0.0)
    bm = xbc[:, SSD_INNER:SSD_INNER + SSD_GROUPS * SSD_STATE].astype(bf16)
    cm = xbc[:, SSD_INNER + SSD_GROUPS * SSD_STATE:].astype(bf16)

    dta = dt * (-jnp.exp(alog_ref[...]))
    cum = jnp.dot(tri_ref[...], dta, preferred_element_type=f32, precision=HIGHEST)
    cum_t = cum.T
    dt_t = dt.T
    cum_last = cum[ll - 1:ll, :]
    e = e_ref[...]
    ecx = _split_dot(jnp.exp(cum), e)
    tex = _split_dot(jnp.exp(cum_last - cum) * dt, e)
    causal = lax.broadcasted_iota(jnp.int32, (ll, ll), 0) >= lax.broadcasted_iota(jnp.int32, (ll, ll), 1)

    ys = []
    for g in range(SSD_GROUPS):
        cg = cm[:, g * SSD_STATE:(g + 1) * SSD_STATE]
        bg = bm[:, g * SSD_STATE:(g + 1) * SSD_STATE]
        cbm = _dot_nt(cg, bg)
        xg = xs[:, g * gw:(g + 1) * gw]
        htg = ht_ref[:, g * gw:(g + 1) * gw]
        y_parts = []
        for j in range(hpg):
            ln = FDT_LANE0 + g * hpg + j
            seg = cum[:, ln:ln + 1] - cum_t[ln:ln + 1, :]
            decay = jnp.where(causal, jnp.exp(jnp.where(causal, seg, 0.0)), 0.0)
            mh = (cbm * decay * dt_t[ln:ln + 1, :]).astype(bf16)
            y_parts.append(_dot(mh, xg[:, j * SSD_HEAD_DIM:(j + 1) * SSD_HEAD_DIM].astype(bf16)))
        yg = jnp.concatenate(y_parts, axis=1) + _dot(cg, htg.astype(bf16)) * ecx[:, g * gw:(g + 1) * gw]
        ys.append(yg)
        xp = (xg * tex[:, g * gw:(g + 1) * gw]).astype(bf16)
        ht_ref[:, g * gw:(g + 1) * gw] = ecx[ll - 1:ll, g * gw:(g + 1) * gw] * htg + _dot_tn(bg, xp)

    y = jnp.concatenate(ys, axis=1) + dexp_ref[...] * xs
    zz = z_ref[...] if lv == ll else jnp.concatenate([z_ref[...], jnp.zeros((ll - lv, SSD_INNER), f32)], axis=0)
    y = y * _silu(zz)
    outs = []
    for g in range(SSD_GROUPS):
        yg = y[:, g * gw:(g + 1) * gw]
        ms = jnp.mean(yg * yg, axis=-1, keepdims=True)
        outs.append(yg * lax.rsqrt(ms + NORM_EPS))
    yn = jnp.concatenate(outs, axis=1) * gn_ref[...]
    y_ref[...] = yn[:lv].astype(y_ref.dtype)

    @pl.when(c == pl.num_programs(1) - 1)
    def _():
        hout_ref[...] = ht_ref[...].T
        cout_ref[...] = conv_tail


def ssd_mix(z, u, fdt, conv_w, conv_b, dt_bias, a_log, d_skip, gnorm, h0, c0, *, lv):
    m = z.shape[0]
    nseq = h0.shape[0]
    ll = SSD_CHUNK
    assert lv <= ll and lv % SUBLANES == 0 and lv >= SSD_CONV - 1 and m % (nseq * lv) == 0
    nchunk = m // (nseq * lv)
    pad = lambda v: jnp.zeros((1, LANES), f32).at[0, FDT_LANE0:FDT_LANE0 + SSD_HEADS].set(v)
    e = np.zeros((LANES, SSD_INNER), np.float32)
    for h in range(SSD_HEADS):
        e[FDT_LANE0 + h, h * SSD_HEAD_DIM:(h + 1) * SSD_HEAD_DIM] = 1.0
    r = np.arange(ll)
    tri = jnp.asarray((r[:, None] >= r[None, :]).astype(np.float32))
    row = lambda i, c: (i * nchunk + c, 0)
    const = lambda i, c: (0, 0)
    seq3 = lambda i, c: (i, 0, 0)
    return pl.pallas_call(
        functools.partial(_ssd_kernel, ll=ll, lv=lv),
        out_shape=(jax.ShapeDtypeStruct((m, SSD_INNER), bf16),
                   jax.ShapeDtypeStruct((nseq, SSD_INNER, SSD_STATE), f32),
                   jax.ShapeDtypeStruct((nseq, SSD_CONV - 1, SSD_CONV_CH), f32)),
        grid=(nseq, nchunk),
        in_specs=[pl.BlockSpec((lv, SSD_INNER), row), pl.BlockSpec((lv, SSD_CONV_CH), row),
                  pl.BlockSpec((lv, LANES), row),
                  pl.BlockSpec((SSD_CONV, SSD_CONV_CH), const), pl.BlockSpec((1, SSD_CONV_CH), const),
                  pl.BlockSpec((1, LANES), const), pl.BlockSpec((1, LANES), const),
                  pl.BlockSpec((1, SSD_INNER), const), pl.BlockSpec((1, SSD_INNER), const),
                  pl.BlockSpec((LANES, SSD_INNER), const), pl.BlockSpec((ll, ll), const),
                  pl.BlockSpec((None, SSD_INNER, SSD_STATE), seq3),
                  pl.BlockSpec((None, SSD_CONV - 1, SSD_CONV_CH), seq3)],
        out_specs=(pl.BlockSpec((lv, SSD_INNER), row),
                   pl.BlockSpec((None, SSD_INNER, SSD_STATE), seq3),
                   pl.BlockSpec((None, SSD_CONV - 1, SSD_CONV_CH), seq3)),
        scratch_shapes=[pltpu.VMEM((SSD_STATE, SSD_INNER), f32), pltpu.VMEM((SUBLANES + ll, SSD_CONV_CH), f32)],
        compiler_params=_params(("arbitrary", "arbitrary")),
        name="ssd_mix",
    )(z, u, fdt, conv_w, conv_b.reshape(1, -1), pad(dt_bias), pad(a_log),
      jnp.repeat(d_skip, SSD_HEAD_DIM).reshape(1, -1), gnorm.reshape(1, -1), jnp.asarray(e, dtype=bf16), tri, h0, c0)


def _hgrn_kernel(q_ref, f_ref, i_ref, g_ref, la_ref, lb_ref, om_ref, gn_ref, tri_ref, s0_ref,
                 o_ref, sout_ref, st_ref, *, ll, lv):
    c = pl.program_id(1)

    @pl.when(c == 0)
    def _():
        for h in range(HEADS):
            st_ref[h] = s0_ref[h].T

    def padded(ref):
        x = ref[...]
        return x if lv == ll else jnp.concatenate([x, jnp.zeros((ll - lv, WIDTH), f32)], axis=0)

    fl = padded(f_ref)
    a = la_ref[...]
    b = lb_ref[...] + _log_sigmoid(fl)
    lf = jnp.maximum(a, b) + jnp.log(1.0 + jnp.exp(-jnp.abs(a - b)))
    key = om_ref[...] * jax.nn.sigmoid(-fl)
    q = _silu(padded(q_ref))
    v = padded(i_ref)
    if lv < ll:
        live = lax.broadcasted_iota(jnp.int32, (ll, WIDTH), 0) < lv
        lf = jnp.where(live, lf, 0.0)
        key = jnp.where(live, key, 0.0)
    cum = jnp.dot(tri_ref[...], lf, preferred_element_type=f32, precision=HIGHEST)
    gate = _silu(padded(g_ref))

    nsub = ll // HG_SUB
    lrow = lax.broadcasted_iota(jnp.int32, (HG_SUB, HEAD_DIM), 0)
    lane = lax.broadcasted_iota(jnp.int32, (HG_SUB, HG_SUB), 1)
    lrow2 = lax.broadcasted_iota(jnp.int32, (HG_SUB, HG_SUB), 0)
    for h in range(HEADS):
        sl = slice(h * HEAD_DIM, (h + 1) * HEAD_DIM)
        qh, kh, vh, ch = q[:, sl], key[:, sl], v[:, sl], cum[:, sl]
        vb = vh.astype(bf16)
        st = st_ref[h]
        c_last = ch[ll - 1:ll]
        o_rows = []
        for blk in range(nsub):
            r0 = blk * HG_SUB
            cb_, qb_, kb_ = ch[r0:r0 + HG_SUB], qh[r0:r0 + HG_SUB], kh[r0:r0 + HG_SUB]
            diag = jnp.zeros((HG_SUB, HG_SUB), f32)
            for s in range(HG_SUB):
                ok = lrow >= s
                e = jnp.exp(jnp.where(ok, cb_ - cb_[s:s + 1], 0.0))
                val = jnp.sum(jnp.where(ok, qb_ * kb_[s:s + 1] * e, 0.0), axis=-1, keepdims=True)
                diag = jnp.where(lane == s, val, diag)
            diag = jnp.where(lane <= lrow2, diag, 0.0)
            parts = []
            if r0 > 0:
                ref_c = ch[r0 - 1:r0]
                qt = (qb_ * jnp.exp(cb_ - ref_c)).astype(bf16)
                kt = (kh[:r0] * jnp.exp(ref_c - ch[:r0])).astype(bf16)
                parts.append(_dot_nt(qt, kt))
            parts.append(diag)
            if r0 + HG_SUB < ll:
                parts.append(jnp.zeros((HG_SUB, ll - r0 - HG_SUB), f32))
            att = jnp.concatenate(parts, axis=1) if len(parts) > 1 else parts[0]
            o_rows.append(_dot(att.astype(bf16), vb))
        oh = jnp.concatenate(o_rows, axis=0) + _dot_nt((qh * jnp.exp(ch)).astype(bf16), st.astype(bf16))
        st_ref[h] = st * jnp.exp(c_last) + _dot_tn(vb, (kh * jnp.exp(c_last - ch)).astype(bf16))
        ms = jnp.mean(oh * oh, axis=-1, keepdims=True)
        on = oh * lax.rsqrt(ms + NORM_EPS) * gn_ref[:, sl] * gate[:, sl]
        o_ref[:, sl] = on[:lv].astype(o_ref.dtype)

    @pl.when(c == pl.num_programs(1) - 1)
    def _():
        for h in range(HEADS):
            sout_ref[h] = st_ref[h].T


def hgrn_mix(proj, lower_bound, gnorm, s0, *, lv):
    m = proj.shape[0]
    nseq = s0.shape[0]
    ll = HG_CHUNK
    assert lv <= ll and lv % SUBLANES == 0 and m % (nseq * lv) == 0
    nchunk = m // (nseq * lv)
    lb = lower_bound.reshape(1, WIDTH).astype(f32)
    r = np.arange(ll)
    tri = jnp.asarray((r[:, None] >= r[None, :]).astype(np.float32))
    col = lambda k: pl.BlockSpec((lv, WIDTH), lambda i, c, k=k: (i * nchunk + c, k))
    const = lambda i, c: (0, 0)
    seq4 = lambda i, c: (i, 0, 0, 0)
    return pl.pallas_call(
        functools.partial(_hgrn_kernel, ll=ll, lv=lv),
        out_shape=(jax.ShapeDtypeStruct((m, WIDTH), bf16),
                   jax.ShapeDtypeStruct((nseq, HEADS, HEAD_DIM, HEAD_DIM), f32)),
        grid=(nseq, nchunk),
        in_specs=[col(0), col(1), col(2), col(3),
                  pl.BlockSpec((1, WIDTH), const), pl.BlockSpec((1, WIDTH), const), pl.BlockSpec((1, WIDTH), const),
                  pl.BlockSpec((1, WIDTH), const), pl.BlockSpec((ll, ll), const),
                  pl.BlockSpec((None, HEADS, HEAD_DIM, HEAD_DIM), seq4)],
        out_specs=(pl.BlockSpec((lv, WIDTH), lambda i, c: (i * nchunk + c, 0)),
                   pl.BlockSpec((None, HEADS, HEAD_DIM, HEAD_DIM), seq4)),
        scratch_shapes=[pltpu.VMEM((HEADS, HEAD_DIM, HEAD_DIM), f32)],
        compiler_params=_params(("arbitrary", "arbitrary")),
        name="hgrn_mix",
    )(proj, proj, proj, proj, jnp.log(lb), jnp.log1p(-lb), 1.0 - lb, gnorm.reshape(1, WIDTH), tri, s0)


def _heads_to_rows(x, nb, n_new):
    return x.reshape(nb, n_new, HEADS, HEAD_DIM).transpose(0, 2, 1, 3).reshape(nb, HEADS * n_new, HEAD_DIM)


def _rows_to_heads(x, nb, n_new):
    return x.reshape(nb, HEADS, n_new, HEAD_DIM).transpose(0, 2, 1, 3).reshape(nb * n_new, WIDTH)


def _token_head_rows(x, nb, n_new):
    return x.reshape(nb, n_new * HEADS, HEAD_DIM)


def _trunk(x, mem_k, mem_v, p, lower_bounds, *, nseq, decode):
    m = x.shape[0]
    t = m // nseq
    assert decode or nseq == 1
    tm = min(m, 1024)
    depth = p["norm_ffn1"].shape[0]
    st = {}
    for layer in range(depth):
        i = layer // 2
        hmid = norm_matmul(x, p["norm_ffn1"][layer], [(p["ffn1_gate"], layer, 0), (p["ffn1_up"], layer, 0)],
                           n=D_FF, tn=512, tm=tm, out_dtype=bf16, swiglu=True, name="ffn1_up")
        x = matmul_residual([hmid], p["ffn1_down"], layer, x, scale=0.5, tm=tm, tn=256, name="ffn1_down")
        gain = p["norm_mix"][layer]
        if layer % 2 == 0:
            w_in = p["ab_w_in"][i]
            qkv = norm_matmul(x, gain, [(w_in, None, 0)], n=3 * WIDTH, tn=512, tm=tm, name="ab_qkv")
            off_z = 3 * WIDTH + HEADS
            off_x = off_z + SSD_INNER
            off_dt = off_x + SSD_CONV_CH
            w_fdt = jnp.concatenate([w_in[:, 3 * WIDTH:off_z], w_in[:, off_dt:off_dt + SSD_HEADS],
                                     jnp.zeros((D_MODEL, LANES - HEADS - SSD_HEADS), f32)], axis=1)
            zz = norm_matmul(x, gain, [(w_in[:, off_z:off_x], None, 0)], n=SSD_INNER, tn=512, tm=tm, name="ab_z")
            uu = norm_matmul(x, gain, [(w_in[:, off_x:off_dt], None, 0)], n=SSD_CONV_CH, tn=512, tm=tm, name="ab_xbc")
            fdt = norm_matmul(x, gain, [(w_fdt, None, 0)], n=LANES, tn=LANES, tm=tm, name="ab_fdt")
            logf, ck = fox_gate(fdt, p["ab_fox_fbias"][i], seg=t, tb=min(m, 512))
            k_new, v_new = qkv[:, WIDTH:2 * WIDTH], qkv[:, 2 * WIDTH:]
            if decode:
                bias_new = (-ck).T.reshape(nseq, 1, t * HEADS)
                o = decode_attention("fox", _heads_to_rows(qkv[:, :WIDTH], nseq, t), _token_head_rows(k_new, nseq, t),
                                     _token_head_rows(v_new, nseq, t), p["cache_fox_k"], p["cache_fox_v"],
                                     p["page_table"], bias_new=bias_new,
                                     bias_past=fox_past_bias(p["cache_fox_logf"], p["page_table"]))
                o_fox = _rows_to_heads(o, nseq, t).astype(bf16)
                h0, c0, lv = p["state_ssd"][i].reshape(nseq, SSD_INNER, SSD_STATE), p["state_ssd_conv"][i], t
            else:
                o_fox = fox_flash(qkv, ck, tq=1024, tk=1024, sub=512)
                h0 = jnp.zeros((nseq, SSD_INNER, SSD_STATE), f32)
                c0 = jnp.zeros((nseq, SSD_CONV - 1, SSD_CONV_CH), f32)
                lv = SSD_CHUNK
            y, h_new, c_new = ssd_mix(zz, uu, fdt, p["ab_conv_w"][i], p["ab_conv_b"][i], p["ab_dt_bias"][i],
                                      p["ab_A_log"][i], p["ab_D"][i], p["ab_ssd_norm"][i], h0, c0, lv=lv)
            x = matmul_residual([o_fox, y], p["ab_w_out"], i, x, scale=1.0, tm=tm, tn=512, name="ab_out")
            st.setdefault("fox_k", []).append(k_new.reshape(nseq, t, HEADS, HEAD_DIM))
            st.setdefault("fox_v", []).append(v_new.reshape(nseq, t, HEADS, HEAD_DIM))
            st.setdefault("fox_logf", []).append(logf.reshape(nseq, t, HEADS))
            st.setdefault("ssd_conv", []).append(c_new)
            st.setdefault("ssd_state", []).append(h_new.reshape(nseq, SSD_HEADS, SSD_HEAD_DIM, SSD_STATE))
        else:
            proj = norm_matmul(x, gain, [(p["cd_w_in"], i, 0)], n=7 * WIDTH, tn=512, tm=tm, name="cd_in")
            k_new, v_new = proj[:, 5 * WIDTH:6 * WIDTH], proj[:, 6 * WIDTH:]
            if decode:
                s0, lv = p["state_hgrn"][i], t
                o = decode_attention("sb", _heads_to_rows(proj[:, 4 * WIDTH:5 * WIDTH], nseq, t),
                                     _token_head_rows(k_new, nseq, t), _token_head_rows(v_new, nseq, t),
                                     p["cache_sb_k"], p["cache_sb_v"], p["page_table"])
                o_sb = _rows_to_heads(o, nseq, t).astype(bf16)
            else:
                s0, lv = jnp.zeros((nseq, HEADS, HEAD_DIM, HEAD_DIM), f32), HG_CHUNK
                o_sb = sb_flash(proj, 4 * HEADS, tq=1024, tk=512, sub=1024)
            o_hg, s_new = hgrn_mix(proj, lower_bounds[layer], p["cd_hg_norm"][i], s0, lv=lv)
            x = matmul_residual([o_hg, o_sb], p["cd_w_out"], i, x, scale=1.0, tm=tm, tn=512, name="cd_out")
            st.setdefault("hgrn", []).append(s_new)
            st.setdefault("sb_k", []).append(k_new.reshape(nseq, t, HEADS, HEAD_DIM))
            st.setdefault("sb_v", []).append(v_new.reshape(nseq, t, HEADS, HEAD_DIM))
        qm = norm_matmul(x, p["norm_mem"][layer], [(p["mem_wq"], layer, 0)], n=MEM_WIDTH, tn=MEM_WIDTH, tm=tm,
                         name="mem_q")
        om = mem_attention(qm, mem_k, mem_v, layer, tq=min(t, 1024))
        x = matmul_residual([om], p["mem_wo"], layer, x, scale=1.0, tm=tm, tn=1024, name="mem_out")
        hmid = norm_matmul(x, p["norm_ffn2"][layer], [(p["ffn2_gate"], layer, 0), (p["ffn2_up"], layer, 0)],
                           n=D_FF, tn=512, tm=tm, out_dtype=bf16, swiglu=True, name="ffn2_up")
        x = matmul_residual([hmid], p["ffn2_down"], layer, x, scale=0.5, tm=tm, tn=256, name="ffn2_down")
    y = rmsnorm_rows(x, p["norm_final"], tm=min(m, 512))
    return y, {k: jnp.stack(v) for k, v in st.items()}


def kernel(x_prompt, x_sample, cache_fox_k, cache_fox_v, cache_fox_logf, state_ssd_conv, state_ssd, state_hgrn, cache_sb_k, cache_sb_v, cache_mem_k, cache_mem_v, page_table, mem_prompt, norm_ffn1, ffn1_gate, ffn1_up, ffn1_down, norm_mix, ab_w_in, ab_fox_fbias, ab_conv_w, ab_conv_b, ab_dt_bias, ab_A_log, ab_D, ab_ssd_norm, ab_w_out, cd_w_in, hg_lower_bound, cd_hg_norm, cd_w_out, norm_mem, norm_memkv, mem_wq, mem_wk, mem_wv, mem_wo, norm_ffn2, ffn2_gate, ffn2_up, ffn2_down, norm_final):
    p = dict(norm_ffn1=norm_ffn1, ffn1_gate=ffn1_gate, ffn1_up=ffn1_up, ffn1_down=ffn1_down, norm_mix=norm_mix,
             ab_w_in=ab_w_in, ab_fox_fbias=ab_fox_fbias, ab_conv_w=ab_conv_w, ab_conv_b=ab_conv_b,
             ab_dt_bias=ab_dt_bias, ab_A_log=ab_A_log, ab_D=ab_D, ab_ssd_norm=ab_ssd_norm, ab_w_out=ab_w_out,
             cd_w_in=cd_w_in, cd_hg_norm=cd_hg_norm, cd_w_out=cd_w_out, norm_mem=norm_mem, mem_wq=mem_wq,
             mem_wo=mem_wo, norm_ffn2=norm_ffn2, ffn2_gate=ffn2_gate, ffn2_up=ffn2_up, ffn2_down=ffn2_down,
             norm_final=norm_final, cache_fox_k=cache_fox_k, cache_fox_v=cache_fox_v, cache_fox_logf=cache_fox_logf,
             cache_sb_k=cache_sb_k, cache_sb_v=cache_sb_v, page_table=page_table, state_ssd=state_ssd,
             state_ssd_conv=state_ssd_conv, state_hgrn=state_hgrn)
    depth = norm_ffn1.shape[0]
    probs = jax.nn.softmax(hg_lower_bound.astype(f32), axis=0)
    lower_bounds = jnp.cumsum(probs, axis=0) - probs[0]

    bp, seq, _ = x_prompt.shape
    bs, dseq, _ = x_sample.shape
    mem_len = mem_prompt.shape[1]
    mem_rows = mem_prompt.reshape(bp * mem_len, D_MODEL)
    mk = [norm_matmul(mem_rows, norm_memkv[l], [(mem_wk, l, 0)], n=MEM_WIDTH, tn=MEM_WIDTH, tm=mem_len, name="mem_k")
          for l in range(depth)]
    mv = [norm_matmul(mem_rows, norm_memkv[l], [(mem_wv, l, 0)], n=MEM_WIDTH, tn=MEM_WIDTH, tm=mem_len, name="mem_v")
          for l in range(depth)]
    mem_k_p = jnp.stack(mk).reshape(depth, bp, mem_len, MEM_WIDTH)
    mem_v_p = jnp.stack(mv).reshape(depth, bp, mem_len, MEM_WIDTH)

    y_p, sp = _trunk(x_prompt.reshape(bp * seq, D_MODEL), mem_k_p, mem_v_p, p, lower_bounds, nseq=bp, decode=False)
    y_s, ss = _trunk(x_sample.reshape(bs * dseq, D_MODEL), cache_mem_k.reshape(depth, bs, mem_len, MEM_WIDTH),
                     cache_mem_v.reshape(depth, bs, mem_len, MEM_WIDTH), p, lower_bounds, nseq=bs, decode=True)

    m5 = (depth, bp, mem_len, MEM_HEADS, HEAD_DIM)
    return (y_p.reshape(bp, seq, D_MODEL), y_s.reshape(bs, dseq, D_MODEL),
            sp["fox_k"], sp["fox_v"], sp["fox_logf"], sp["ssd_conv"], sp["ssd_state"], sp["hgrn"], sp["sb_k"], sp["sb_v"],
            mem_k_p.reshape(m5), mem_v_p.reshape(m5),
            ss["fox_k"], ss["fox_v"], ss["fox_logf"], ss["ssd_conv"], ss["ssd_state"], ss["hgrn"], ss["sb_k"], ss["sb_v"])
```

```python
import functools
import math

import numpy as np
import jax
import jax.numpy as jnp
from jax import lax
from jax.experimental import pallas as pl
from jax.experimental.pallas import tpu as pltpu

f32 = jnp.float32
bf16 = jnp.bfloat16
HIGHEST = lax.Precision.HIGHEST

D_MODEL = 2048
D_FF = 5632
HEADS = 8
HEAD_DIM = 128
WIDTH = HEADS * HEAD_DIM
SSD_HEADS = 32
SSD_HEAD_DIM = 64
SSD_INNER = SSD_HEADS * SSD_HEAD_DIM
SSD_GROUPS = 4
SSD_STATE = 128
SSD_CONV = 4
SSD_CONV_CH = SSD_INNER + 2 * SSD_GROUPS * SSD_STATE
SSD_CHUNK = 128
HG_CHUNK = 64
HG_SUB = 16
MEM_HEADS = 4
MEM_WIDTH = MEM_HEADS * HEAD_DIM
PAGE = 128
NORM_EPS = 1e-6
NEG_INF = -1e30
ATTN_SCALE = HEAD_DIM ** -0.5
LOG2E = math.log2(math.e)

LANES = 128
SUBLANES = 8
VMEM_LIMIT_BYTES = 56 * 1024 * 1024
FDT_LANE0 = 8

NT_DIMS = (((1,), (1,)), ((), ()))
TN_DIMS = (((0,), (0,)), ((), ()))


def _params(sem):
    return pltpu.CompilerParams(dimension_semantics=sem, vmem_limit_bytes=VMEM_LIMIT_BYTES)


def _dot(a, b):
    return jnp.dot(a, b, preferred_element_type=f32)


def _dot_nt(a, b):
    return lax.dot_general(a, b, NT_DIMS, preferred_element_type=f32)


def _dot_tn(a, b):
    return lax.dot_general(a, b, TN_DIMS, preferred_element_type=f32)


def _split_dot(x, e):
    hi = x.astype(bf16)
    lo = (x - hi.astype(f32)).astype(bf16)
    return _dot(jnp.concatenate([hi, lo], axis=1), jnp.concatenate([e, e], axis=0))


def _silu(x):
    return x * jax.nn.sigmoid(x)


def _log_sigmoid(x):
    return jnp.minimum(x, 0.0) - jnp.log(1.0 + jnp.exp(-jnp.abs(x)))


def _nmm_kernel(x_ref, g_ref, *refs, n_w, swiglu):
    w_refs, o_ref, xn_ref = refs[:n_w], refs[n_w], refs[n_w + 1]

    @pl.when(pl.program_id(1) == 0)
    def _():
        x = x_ref[...]
        ms = jnp.mean(x * x, axis=-1, keepdims=True)
        xn_ref[...] = (x * lax.rsqrt(ms + NORM_EPS) * g_ref[...]).astype(bf16)

    xn = xn_ref[...]
    if swiglu:
        g = _dot(xn, w_refs[0][...].astype(bf16))
        u = _dot(xn, w_refs[1][...].astype(bf16))
        o_ref[...] = (_silu(g) * u).astype(o_ref.dtype)
    else:
        o_ref[...] = _dot(xn, w_refs[0][...].astype(bf16)).astype(o_ref.dtype)


def _w_spec(w, lead, k, tn, off):
    if w.ndim == 3:
        return pl.BlockSpec((None, k, tn), lambda i, j: (lead, 0, j + off))
    return pl.BlockSpec((k, tn), lambda i, j: (0, j + off))


def norm_matmul(x, gain, ws, *, n, tn, tm, out_dtype=f32, swiglu=False, name="nmm"):
    m, k = x.shape
    assert m % tm == 0 and n % tn == 0
    in_specs = [pl.BlockSpec((tm, k), lambda i, j: (i, 0)), pl.BlockSpec((1, k), lambda i, j: (0, 0))]
    in_specs += [_w_spec(w, lead, k, tn, off) for (w, lead, off) in ws]
    return pl.pallas_call(
        functools.partial(_nmm_kernel, n_w=len(ws), swiglu=swiglu),
        out_shape=jax.ShapeDtypeStruct((m, n), out_dtype),
        grid=(m // tm, n // tn),
        in_specs=in_specs,
        out_specs=pl.BlockSpec((tm, tn), lambda i, j: (i, j)),
        scratch_shapes=[pltpu.VMEM((tm, k), bf16)],
        compiler_params=_params(("arbitrary", "arbitrary")),
        name=name,
    )(x, gain.reshape(1, k), *[w for (w, _, _) in ws])


def _mmr_kernel(*refs, widths, scale):
    a_refs = refs[:len(widths)]
    w_ref, r_ref, o_ref = refs[len(widths):]
    w = w_ref[...].astype(bf16)
    p, off = None, 0
    for a_ref, kw in zip(a_refs, widths):
        d = _dot(a_ref[...].astype(bf16), w[off:off + kw])
        p = d if p is None else p + d
        off += kw
    o_ref[...] = r_ref[...] + scale * p


def matmul_residual(a_list, w, lead, res, *, scale, tm, tn, name="mmr"):
    m = a_list[0].shape[0]
    widths = tuple(a.shape[1] for a in a_list)
    kk = sum(widths)
    n = res.shape[1]
    assert m % tm == 0 and n % tn == 0 and w.shape[-2] == kk
    if w.ndim == 3:
        w_spec = pl.BlockSpec((None, kk, tn), lambda i, j: (lead, 0, j))
    else:
        w_spec = pl.BlockSpec((kk, tn), lambda i, j: (0, j))
    return pl.pallas_call(
        functools.partial(_mmr_kernel, widths=widths, scale=scale),
        out_shape=jax.ShapeDtypeStruct((m, n), f32),
        grid=(m // tm, n // tn),
        in_specs=[pl.BlockSpec((tm, kw), lambda i, j: (i, 0)) for kw in widths] + [
            w_spec, pl.BlockSpec((tm, tn), lambda i, j: (i, j))],
        out_specs=pl.BlockSpec((tm, tn), lambda i, j: (i, j)),
        compiler_params=_params(("arbitrary", "arbitrary")),
        name=name,
    )(*a_list, w, res)


def _rmsnorm_kernel(x_ref, g_ref, o_ref):
    x = x_ref[...]
    ms = jnp.mean(x * x, axis=-1, keepdims=True)
    o_ref[...] = x * lax.rsqrt(ms + NORM_EPS) * g_ref[...]


def rmsnorm_rows(x, gain, *, tm):
    m, k = x.shape
    return pl.pallas_call(
        _rmsnorm_kernel,
        out_shape=jax.ShapeDtypeStruct((m, k), f32),
        grid=(m // tm,),
        in_specs=[pl.BlockSpec((tm, k), lambda i: (i, 0)), pl.BlockSpec((1, k), lambda i: (0, 0))],
        out_specs=pl.BlockSpec((tm, k), lambda i: (i, 0)),
        compiler_params=_params(("arbitrary",)),
        name="final_norm",
    )(x, gain.reshape(1, k))


def _fox_gate_kernel(fdt_ref, bias_ref, u_ref, lf_ref, ck_ref, carry_ref, *, tb, seg, carry):
    lf = _log_sigmoid(fdt_ref[...] + bias_ref[...])
    lf_ref[...] = lf[:, :HEADS]
    lft = lf.T[:HEADS, :]
    cum = jnp.dot(lft, u_ref[...], preferred_element_type=f32, precision=HIGHEST)
    if carry:
        @pl.when((pl.program_id(0) * tb) % seg == 0)
        def _():
            carry_ref[...] = jnp.zeros_like(carry_ref)
        cum = cum + carry_ref[:, 0:1]
        carry_ref[...] = jnp.broadcast_to(cum[:, tb - 1:tb], carry_ref.shape)
    ck_ref[...] = cum


def fox_gate(fdt, fbias, *, seg, tb):
    m = fdt.shape[0]
    assert m % tb == 0 and (seg % tb == 0 or tb % seg == 0)
    r = np.arange(tb)
    u = ((r[:, None] <= r[None, :]) & (r[:, None] // seg == r[None, :] // seg)).astype(np.float32)
    bias = jnp.zeros((1, LANES), f32).at[0, :HEADS].set(fbias)
    return pl.pallas_call(
        functools.partial(_fox_gate_kernel, tb=tb, seg=seg, carry=seg > tb),
        out_shape=(jax.ShapeDtypeStruct((m, HEADS), f32), jax.ShapeDtypeStruct((HEADS, m), f32)),
        grid=(m // tb,),
        in_specs=[pl.BlockSpec((tb, LANES), lambda i: (i, 0)), pl.BlockSpec((1, LANES), lambda i: (0, 0)),
                  pl.BlockSpec((tb, tb), lambda i: (0, 0))],
        out_specs=(pl.BlockSpec((tb, HEADS), lambda i: (i, 0)), pl.BlockSpec((HEADS, tb), lambda i: (0, i))),
        scratch_shapes=[pltpu.VMEM((HEADS, LANES), f32)],
        compiler_params=_params(("arbitrary",)),
        name="fox_gate",
    )(fdt, bias, jnp.asarray(u))


def _lanes(x, width):
    if width <= LANES:
        return x[:, :width]
    return jnp.concatenate([x] * (width // LANES), axis=1)


def _softmax_step(s, vb, state):
    m_prev, l_prev, acc = state
    m_new = jnp.maximum(m_prev, jnp.max(s, axis=-1, keepdims=True))
    alpha = jnp.exp2(m_prev - m_new)
    p = jnp.exp2(s - _lanes(m_new, s.shape[1]))
    l_new = alpha * l_prev + jnp.sum(p, axis=-1, keepdims=True)
    return m_new, l_new, alpha * acc + _dot(p.astype(bf16), vb)


def _softmax_update(s, vb, m_ref, l_ref, acc_ref, rows):
    m_ref[rows, :], l_ref[rows, :], acc_ref[rows, :] = _softmax_step(
        s, vb, (m_ref[rows, :], l_ref[rows, :], acc_ref[rows, :]))


def _fox_flash_kernel(q_ref, k_ref, v_ref, ck_ref, o_ref, qs_ref, m_ref, l_ref, acc_ref, *, tq, tk, sub):
    i = pl.program_id(1)
    nkb = tq // tk
    qs_ref[...] = (q_ref[...] * (ATTN_SCALE * LOG2E)).astype(bf16)
    m_ref[...] = jnp.full(m_ref.shape, NEG_INF, f32)
    l_ref[...] = jnp.zeros(l_ref.shape, f32)
    acc_ref[...] = jnp.zeros(acc_ref.shape, f32)

    def step(j, diag):
        start = pl.multiple_of(j * tk, tk)
        kb = k_ref[pl.ds(start, tk), :].astype(bf16)
        vb = v_ref[pl.ds(start, tk), :].astype(bf16)
        bias = ck_ref[:, pl.ds(start, tk)]
        for r0 in range(0, tq, sub):
            masked = False
            if diag is not None:
                if diag * tk > r0 + sub - 1:
                    continue
                masked = diag * tk + tk - 1 > r0
            rows = slice(r0, r0 + sub)
            s = _dot_nt(qs_ref[rows, :], kb) - bias
            if masked:
                row = lax.broadcasted_iota(jnp.int32, (sub, tk), 0) + r0
                col = lax.broadcasted_iota(jnp.int32, (sub, tk), 1) + diag * tk
                s = jnp.where(col <= row, s, NEG_INF)
            _softmax_update(s, vb, m_ref, l_ref, acc_ref, rows)

    def body(j, c):
        step(j, None)
        return c

    lax.fori_loop(0, i * nkb, body, 0)
    for d in range(nkb):
        step(i * nkb + d, d)
    o_ref[...] = (acc_ref[...] / l_ref[...]).astype(o_ref.dtype)


def fox_flash(qkv, ck, *, tq, tk, sub):
    t = qkv.shape[0]
    assert t % tq == 0 and tq % tk == 0 and tq % sub == 0 and tk % LANES == 0
    stat = pltpu.VMEM((tq, LANES), f32)
    return pl.pallas_call(
        functools.partial(_fox_flash_kernel, tq=tq, tk=tk, sub=sub),
        out_shape=jax.ShapeDtypeStruct((t, WIDTH), bf16),
        grid=(HEADS, t // tq),
        in_specs=[pl.BlockSpec((tq, HEAD_DIM), lambda h, i: (i, h)),
                  pl.BlockSpec((t, HEAD_DIM), lambda h, i: (0, HEADS + h)),
                  pl.BlockSpec((t, HEAD_DIM), lambda h, i: (0, 2 * HEADS + h)),
                  pl.BlockSpec((None, 1, t), lambda h, i: (h, 0, 0))],
        out_specs=pl.BlockSpec((tq, HEAD_DIM), lambda h, i: (i, h)),
        scratch_shapes=[pltpu.VMEM((tq, HEAD_DIM), bf16), stat, stat, pltpu.VMEM((tq, HEAD_DIM), f32)],
        compiler_params=_params(("arbitrary", "arbitrary")),
        name="fox_flash",
    )(qkv, qkv, qkv, (ck * LOG2E).reshape(HEADS, 1, t))


def _sb_tile(z, valid, tri_ext, r):
    nr, sk = z.shape
    cw = tri_ext.shape[0]
    lk = -(jnp.maximum(z, 0.0) + jnp.log2(1.0 + jnp.exp2(-jnp.abs(z))))
    if valid is not None:
        lk = jnp.where(valid, lk, 0.0)
    nch = sk // cw
    stacked = lk if nch == 1 else jnp.concatenate([lk[:, c * cw:(c + 1) * cw] for c in range(nch)], axis=0)
    inc = _split_dot(stacked, tri_ext)
    sums = [None] * nch
    for c in range(nch - 1, -1, -1):
        blk = inc[c * nr:(c + 1) * nr]
        sums[c] = blk[:, :cw] + _lanes(r, cw)
        r = r + blk[:, cw:]
    w = jnp.exp2(z + (sums[0] if nch == 1 else jnp.concatenate(sums, axis=1)))
    if valid is not None:
        w = jnp.where(valid, w, 0.0)
    return w, r


def _sb_flash_kernel(q_ref, k_ref, v_ref, tri_ref, o_ref, qs_ref, r_ref, acc_ref, *, tq, tk, sub):
    i = pl.program_id(1)
    nkb = tq // tk
    qs_ref[...] = (q_ref[...] * (ATTN_SCALE * LOG2E)).astype(bf16)
    r_ref[...] = jnp.zeros(r_ref.shape, f32)
    acc_ref[...] = jnp.zeros(acc_ref.shape, f32)

    def step(j, diag):
        start = pl.multiple_of(j * tk, tk)
        kb = k_ref[pl.ds(start, tk), :].astype(bf16)
        vb = v_ref[pl.ds(start, tk), :].astype(bf16)
        for r0 in range(0, tq, sub):
            valid = None
            if diag is not None:
                if diag * tk >= r0 + sub - 1:
                    continue
                if diag * tk + tk - 1 >= r0:
                    row = lax.broadcasted_iota(jnp.int32, (sub, tk), 0) + r0
                    col = lax.broadcasted_iota(jnp.int32, (sub, tk), 1) + diag * tk
                    valid = col < row
            rows = slice(r0, r0 + sub)
            w, r_new = _sb_tile(_dot_nt(qs_ref[rows, :], kb), valid, tri_ref[...], r_ref[rows, :])
            acc_ref[rows, :] += _dot(w.astype(bf16), vb)
            r_ref[rows, :] = r_new

    for d in range(nkb - 1, -1, -1):
        step(i * nkb + d, d)

    def body(t, c):
        step(i * nkb - 1 - t, None)
        return c

    lax.fori_loop(0, i * nkb, body, 0)
    o_ref[...] = acc_ref[...].astype(o_ref.dtype)


def _tri_ext(n):
    r = np.arange(n)
    tri = (r[:, None] >= r[None, :]).astype(np.float32)
    return jnp.asarray(np.concatenate([tri, np.ones((n, LANES), np.float32)], axis=1), dtype=bf16)


def sb_flash(proj, col0, *, tq, tk, sub):
    t = proj.shape[0]
    assert t % tq == 0 and tq % tk == 0 and tq % sub == 0 and tk % LANES == 0
    return pl.pallas_call(
        functools.partial(_sb_flash_kernel, tq=tq, tk=tk, sub=sub),
        out_shape=jax.ShapeDtypeStruct((t, WIDTH), bf16),
        grid=(HEADS, t // tq),
        in_specs=[pl.BlockSpec((tq, HEAD_DIM), lambda h, i: (i, col0 + h)),
                  pl.BlockSpec((t, HEAD_DIM), lambda h, i: (0, col0 + HEADS + h)),
                  pl.BlockSpec((t, HEAD_DIM), lambda h, i: (0, col0 + 2 * HEADS + h)),
                  pl.BlockSpec((LANES, 2 * LANES), lambda h, i: (0, 0))],
        out_specs=pl.BlockSpec((tq, HEAD_DIM), lambda h, i: (i, h)),
        scratch_shapes=[pltpu.VMEM((tq, HEAD_DIM), bf16), pltpu.VMEM((tq, LANES), f32),
                        pltpu.VMEM((tq, HEAD_DIM), f32)],
        compiler_params=_params(("arbitrary", "arbitrary")),
        name="sb_flash",
    )(proj, proj, proj, _tri_ext(LANES))


PAGE_TILE_ROWS = PAGE * HEADS // LANES
TOK_PER_ROW = LANES // HEADS


def _past_bias_kernel(pt_ref, lf_hbm, dm_ref, hm_ref, up_ref, o_ref, buf, sem, *, n_pages):
    b = pl.program_id(0)

    def copy(j):
        return pltpu.make_async_copy(lf_hbm.at[pt_ref[b, j]], buf.at[j], sem)

    def start(j, c):
        copy(j).start()
        return c

    def wait(j, c):
        copy(j).wait()
        return c

    lax.fori_loop(0, n_pages, start, 0)
    lax.fori_loop(0, n_pages, wait, 0)

    def hdot(a, m_ref):
        return jnp.dot(a, m_ref[...], preferred_element_type=f32, precision=HIGHEST)

    ys = [buf[:, r, :] for r in range(PAGE_TILE_ROWS)]
    later = [None] * PAGE_TILE_ROWS
    acc = jnp.zeros_like(ys[0])
    for r in range(PAGE_TILE_ROWS - 1, -1, -1):
        later[r] = acc
        acc = acc + ys[r]
    pages_after = hdot(jnp.dot(up_ref[...], acc, preferred_element_type=f32, precision=HIGHEST), hm_ref)
    for r in range(PAGE_TILE_ROWS):
        o_ref[:, r, :] = hdot(ys[r], dm_ref) + hdot(later[r], hm_ref) + pages_after


def fox_past_bias(cache_logf, page_table):
    nb, n_pages = page_table.shape
    n_pool = cache_logf.shape[1]
    lane = np.arange(LANES)
    same_head = lane[:, None] % HEADS == lane[None, :] % HEADS
    dm = (same_head & (lane[:, None] // HEADS > lane[None, :] // HEADS)).astype(np.float32)
    hm = same_head.astype(np.float32)
    pg = np.arange(n_pages)
    up = (pg[None, :] > pg[:, None]).astype(np.float32)
    const = lambda shape: pl.BlockSpec(shape, lambda b, pt: (0, 0))
    return pl.pallas_call(
        functools.partial(_past_bias_kernel, n_pages=n_pages),
        out_shape=jax.ShapeDtypeStruct((nb, n_pages, PAGE_TILE_ROWS, LANES), f32),
        grid_spec=pltpu.PrefetchScalarGridSpec(
            num_scalar_prefetch=1,
            grid=(nb,),
            in_specs=[pl.BlockSpec(memory_space=pl.ANY), const((LANES, LANES)), const((LANES, LANES)),
                      const((n_pages, n_pages))],
            out_specs=pl.BlockSpec((None, n_pages, PAGE_TILE_ROWS, LANES), lambda b, pt: (b, 0, 0, 0)),
            scratch_shapes=[pltpu.VMEM((n_pages, PAGE_TILE_ROWS, LANES), f32), pltpu.SemaphoreType.DMA(())],
        ),
        compiler_params=_params(("arbitrary",)),
        name="fox_past_bias",
    )(page_table, cache_logf.reshape(n_pool, PAGE_TILE_ROWS, LANES), jnp.asarray(dm), jnp.asarray(hm), jnp.asarray(up))


def _decode_kernel(pt_ref, q_ref, kn_ref, vn_ref, hm_ref, *refs, mode, pp, n_new):
    k_refs, v_refs = refs[:pp], refs[pp:2 * pp]
    refs = refs[2 * pp:]
    if mode == "fox":
        bn_ref, ap_refs = refs[0], refs[1:1 + pp]
        o_ref = refs[1 + pp]
        state_refs = refs[2 + pp:]
    else:
        tri_ref = refs[0]
        o_ref = refs[1]
        state_refs = refs[2:]
    j = pl.program_id(1)
    rows = HEADS * n_new
    page_cols = PAGE * HEADS
    qs = (q_ref[...] * (ATTN_SCALE * LOG2E)).astype(bf16)

    def own_head(cols):
        rh = lax.broadcasted_iota(jnp.int32, (rows, cols), 0) // n_new
        ch = lax.broadcasted_iota(jnp.int32, (rows, cols), 1) % HEADS
        return rh == ch

    def fox_block(kbs, vbs, biases, valid, state):
        m_prev, l_prev, acc = state
        ss = [jnp.where(valid, _dot_nt(qs, kb) + bias, NEG_INF) for kb, bias in zip(kbs, biases)]
        m_new = m_prev
        for s in ss:
            m_new = jnp.maximum(m_new, jnp.max(s, axis=-1, keepdims=True))
        alpha = jnp.exp2(m_prev - m_new)
        l_new, acc = alpha * l_prev, alpha * acc
        for s, vb in zip(ss, vbs):
            p = jnp.exp2(s - _lanes(m_new, s.shape[1]))
            l_new = l_new + jnp.sum(p, axis=-1, keepdims=True)
            acc = acc + _dot(p.astype(bf16), vb)
        return m_new, l_new, acc

    def sb_block(kbs, vbs, valid, state):
        r, acc = state
        cols = kbs[0].shape[0]
        cw = min(cols, LANES)
        tri = tri_ref[...] if cw == LANES else jnp.concatenate([tri_ref[:cw, :cw], tri_ref[:cw, LANES:]], axis=1)
        n = len(kbs)
        z = jnp.concatenate([_dot_nt(qs, kb) for kb in kbs], axis=1) if n > 1 else _dot_nt(qs, kbs[0])
        w, r = _sb_tile(z, jnp.concatenate([valid] * n, axis=1) if n > 1 else valid, tri, r)
        for p, vb in enumerate(vbs):
            acc = acc + _dot(w[:, p * cols:(p + 1) * cols].astype(bf16), vb)
        return r, acc

    @pl.when(j == 0)
    def _():
        cols = n_new * HEADS
        tq = lax.broadcasted_iota(jnp.int32, (rows, cols), 0) % n_new
        sk = lax.broadcasted_iota(jnp.int32, (rows, cols), 1) // HEADS
        kb, vb = kn_ref[...].astype(bf16), vn_ref[...].astype(bf16)
        zeros = jnp.zeros((rows, LANES), f32)
        if mode == "fox":
            state = fox_block([kb], [vb], [bn_ref[...] * LOG2E], own_head(cols) & (sk <= tq),
                              (jnp.full((rows, LANES), NEG_INF, f32), zeros, zeros))
        else:
            state = sb_block([kb], [vb], own_head(cols) & (sk < tq), (zeros, zeros))
        for ref, val in zip(state_refs, state):
            ref[...] = val

    kbs = [k_refs[p][...].reshape(page_cols, HEAD_DIM).astype(bf16) for p in range(pp)]
    vbs = [v_refs[p][...].reshape(page_cols, HEAD_DIM).astype(bf16) for p in range(pp)]
    state = tuple(ref[...] for ref in state_refs)
    if mode == "fox":
        biases = []
        for p in range(pp):
            ap = ap_refs[p][...] * LOG2E
            biases.append(jnp.concatenate(
                [jnp.broadcast_to(ap[r:r + 1], (rows, LANES)) for r in range(PAGE_TILE_ROWS)], axis=1))
        state = fox_block(kbs, vbs, biases, hm_ref[...] > 0.5, state)
    else:
        state = sb_block(kbs, vbs, hm_ref[...] > 0.5, state)
    for ref, val in zip(state_refs, state):
        ref[...] = val

    @pl.when(j == pl.num_programs(1) - 1)
    def _():
        if mode == "fox":
            o_ref[...] = state[2] / state[1]
        else:
            o_ref[...] = state[1]


def decode_attention(mode, q, k_new, v_new, cache_k, cache_v, page_table, *, bias_new=None, bias_past=None, pp=8):
    nb, rows, _ = q.shape
    n_new = rows // HEADS
    n_pages = page_table.shape[1]
    assert n_pages % pp == 0
    nsteps = n_pages // pp

    def page_of(b, j, pt, p):
        return pt[b, (nsteps - 1 - j) * pp + p]

    kv_specs = [pl.BlockSpec((None, None, PAGE, HEADS, HEAD_DIM),
                             lambda b, j, pt, p=p: (0, page_of(b, j, pt, p), 0, 0, 0)) for p in range(pp)]
    new_spec = pl.BlockSpec((None, rows, HEAD_DIM), lambda b, j, pt: (b, 0, 0))
    own_head = (np.arange(rows)[:, None] // n_new == np.arange(PAGE * HEADS)[None, :] % HEADS).astype(np.float32)
    in_specs = [new_spec, new_spec, new_spec, pl.BlockSpec((rows, PAGE * HEADS), lambda b, j, pt: (0, 0))]
    in_specs += kv_specs + kv_specs
    args = [q, k_new, v_new, jnp.asarray(own_head)] + [cache_k] * pp + [cache_v] * pp
    stat = pltpu.VMEM((rows, LANES), f32)
    if mode == "fox":
        in_specs.append(pl.BlockSpec((None, 1, rows), lambda b, j, pt: (b, 0, 0)))
        in_specs += [pl.BlockSpec((None, None, PAGE_TILE_ROWS, LANES),
                                  lambda b, j, pt, p=p: (b, (nsteps - 1 - j) * pp + p, 0, 0)) for p in range(pp)]
        args += [bias_new] + [bias_past] * pp
        scratch = [stat, stat, pltpu.VMEM((rows, HEAD_DIM), f32)]
    else:
        in_specs.append(pl.BlockSpec((LANES, 2 * LANES), lambda b, j, pt: (0, 0)))
        args.append(_tri_ext(LANES))
        scratch = [stat, pltpu.VMEM((rows, HEAD_DIM), f32)]
    return pl.pallas_call(
        functools.partial(_decode_kernel, mode=mode, pp=pp, n_new=n_new),
        out_shape=jax.ShapeDtypeStruct((nb, rows, HEAD_DIM), f32),
        grid_spec=pltpu.PrefetchScalarGridSpec(
            num_scalar_prefetch=1,
            grid=(nb, nsteps),
            in_specs=in_specs,
            out_specs=pl.BlockSpec((None, rows, HEAD_DIM), lambda b, j, pt: (b, 0, 0)),
            scratch_shapes=scratch,
        ),
        compiler_params=_params(("arbitrary", "arbitrary")),
        name=mode + "_decode",
    )(page_table, *args)


def _mem_attn_kernel(q_ref, k_ref, v_ref, o_ref):
    s = _dot_nt(q_ref[...].astype(bf16), k_ref[...].astype(bf16)) * ATTN_SCALE
    m = jnp.max(s, axis=-1, keepdims=True)
    p = jnp.exp(s - m)
    l = jnp.sum(p, axis=-1, keepdims=True)
    o_ref[...] = (_dot(p.astype(bf16), v_ref[...].astype(bf16)) / l).astype(o_ref.dtype)


def mem_attention(q, mem_k, mem_v, layer, *, tq):
    m = q.shape[0]
    nb, mem_len = mem_k.shape[1], mem_k.shape[2]
    per_b = m // nb
    assert per_b % tq == 0
    nq = per_b // tq
    kv_spec = pl.BlockSpec((None, None, mem_len, HEAD_DIM), lambda i, h: (layer, i // nq, 0, h))
    return pl.pallas_call(
        _mem_attn_kernel,
        out_shape=jax.ShapeDtypeStruct((m, MEM_WIDTH), bf16),
        grid=(m // tq, MEM_HEADS),
        in_specs=[pl.BlockSpec((tq, HEAD_DIM), lambda i, h: (i, h)), kv_spec, kv_spec],
        out_specs=pl.BlockSpec((tq, HEAD_DIM), lambda i, h: (i, h)),
        compiler_params=_params(("arbitrary", "arbitrary")),
        name="mem_attn",
    )(q, mem_k, mem_v)


def _ssd_kernel(z_ref, u_ref, fdt_ref, cw_ref, cb_ref, dtb_ref, alog_ref, dexp_ref, gn_ref, e_ref, tri_ref,
                h0_ref, c0_ref, y_ref, hout_ref, cout_ref, ht_ref, cbuf_ref, *, ll, lv):
    c = pl.program_id(1)
    npad = SSD_CONV - 1
    base = SUBLANES
    hpg = SSD_HEADS // SSD_GROUPS
    gw = hpg * SSD_HEAD_DIM

    @pl.when(c == 0)
    def _():
        ht_ref[...] = h0_ref[...].T
        cbuf_ref[base - npad:base, :] = c0_ref[...]
        if lv < ll:
            cbuf_ref[base:base + ll, :] = jnp.zeros((ll, SSD_CONV_CH), f32)

    cbuf_ref[base:base + lv, :] = u_ref[...]
    xbc = cb_ref[...] + sum(cw_ref[i:i + 1, :] * cbuf_ref[base - npad + i:base - npad + i + ll, :]
                            for i in range(SSD_CONV))
    conv_tail = cbuf_ref[base + lv - npad:base + lv, :]
    cbuf_ref[base - npad:base, :] = conv_tail
    xbc = _silu(xbc)

    fdt = fdt_ref[...] if lv == ll else jnp.concatenate([fdt_ref[...], jnp.zeros((ll - lv, LANES), f32)], axis=0)
    dt = jax.nn.softplus(fdt + dtb_ref[...])
    lane = lax.broadcasted_iota(jnp.int32, (ll, LANES), 1)
    row = lax.broadcasted_iota(jnp.int32, (ll, LANES), 0)
    live = (lane >= FDT_LANE0) & (lane < FDT_LANE0 + SSD_HEADS) & (row < lv)
    dt = jnp.where(live, dt, 0.0)
    xs = xbc[:, :SSD_INNER]
    if lv < ll:
        xs = jnp.where(lax.broadcasted_iota(jnp.int32, (ll, SSD_INNER), 0) < lv, xs, 0.0)
    bm = xbc[:, SSD_INNER:SSD_INNER + SSD_GROUPS * SSD_STATE].astype(bf16)
    cm = xbc[:, SSD_INNER + SSD_GROUPS * SSD_STATE:].astype(bf16)

    dta = dt * (-jnp.exp(alog_ref[...]))
    cum = jnp.dot(tri_ref[...], dta, preferred_element_type=f32, precision=HIGHEST)
    cum_t = cum.T
    dt_t = dt.T
    cum_last = cum[ll - 1:ll, :]
    e = e_ref[...]
    ecx = _split_dot(jnp.exp(cum), e)
    tex = _split_dot(jnp.exp(cum_last - cum) * dt, e)
    causal = lax.broadcasted_iota(jnp.int32, (ll, ll), 0) >= lax.broadcasted_iota(jnp.int32, (ll, ll), 1)

    ys = []
    for g in range(SSD_GROUPS):
        cg = cm[:, g * SSD_STATE:(g + 1) * SSD_STATE]
        bg = bm[:, g * SSD_STATE:(g + 1) * SSD_STATE]
        cbm = _dot_nt(cg, bg)
        xg = xs[:, g * gw:(g + 1) * gw]
        htg = ht_ref[:, g * gw:(g + 1) * gw]
        y_parts = []
        for j in range(hpg):
            ln = FDT_LANE0 + g * hpg + j
            seg = cum[:, ln:ln + 1] - cum_t[ln:ln + 1, :]
            decay = jnp.where(causal, jnp.exp(jnp.where(causal, seg, 0.0)), 0.0)
            mh = (cbm * decay * dt_t[ln:ln + 1, :]).astype(bf16)
            y_parts.append(_dot(mh, xg[:, j * SSD_HEAD_DIM:(j + 1) * SSD_HEAD_DIM].astype(bf16)))
        yg = jnp.concatenate(y_parts, axis=1) + _dot(cg, htg.astype(bf16)) * ecx[:, g * gw:(g + 1) * gw]
        ys.append(yg)
        xp = (xg * tex[:, g * gw:(g + 1) * gw]).astype(bf16)
        ht_ref[:, g * gw:(g + 1) * gw] = ecx[ll - 1:ll, g * gw:(g + 1) * gw] * htg + _dot_tn(bg, xp)

    y = jnp.concatenate(ys, axis=1) + dexp_ref[...] * xs
    zz = z_ref[...] if lv == ll else jnp.concatenate([z_ref[...], jnp.zeros((ll - lv, SSD_INNER), f32)], axis=0)
    y = y * _silu(zz)
    outs = []
    for g in range(SSD_GROUPS):
        yg = y[:, g * gw:(g + 1) * gw]
        ms = jnp.mean(yg * yg, axis=-1, keepdims=True)
        outs.append(yg * lax.rsqrt(ms + NORM_EPS))
    yn = jnp.concatenate(outs, axis=1) * gn_ref[...]
    y_ref[...] = yn[:lv].astype(y_ref.dtype)

    @pl.when(c == pl.num_programs(1) - 1)
    def _():
        hout_ref[...] = ht_ref[...].T
        cout_ref[...] = conv_tail


def ssd_mix(z, u, fdt, conv_w, conv_b, dt_bias, a_log, d_skip, gnorm, h0, c0, *, lv):
    m = z.shape[0]
    nseq = h0.shape[0]
    ll = SSD_CHUNK
    assert lv <= ll and lv % SUBLANES == 0 and lv >= SSD_CONV - 1 and m % (nseq * lv) == 0
    nchunk = m // (nseq * lv)
    pad = lambda v: jnp.zeros((1, LANES), f32).at[0, FDT_LANE0:FDT_LANE0 + SSD_HEADS].set(v)
    e = np.zeros((LANES, SSD_INNER), np.float32)
    for h in range(SSD_HEADS):
        e[FDT_LANE0 + h, h * SSD_HEAD_DIM:(h + 1) * SSD_HEAD_DIM] = 1.0
    r = np.arange(ll)
    tri = jnp.asarray((r[:, None] >= r[None, :]).astype(np.float32))
    row = lambda i, c: (i * nchunk + c, 0)
    const = lambda i, c: (0, 0)
    seq3 = lambda i, c: (i, 0, 0)
    return pl.pallas_call(
        functools.partial(_ssd_kernel, ll=ll, lv=lv),
        out_shape=(jax.ShapeDtypeStruct((m, SSD_INNER), bf16),
                   jax.ShapeDtypeStruct((nseq, SSD_INNER, SSD_STATE), f32),
                   jax.ShapeDtypeStruct((nseq, SSD_CONV - 1, SSD_CONV_CH), f32)),
        grid=(nseq, nchunk),
        in_specs=[pl.BlockSpec((lv, SSD_INNER), row), pl.BlockSpec((lv, SSD_CONV_CH), row),
                  pl.BlockSpec((lv, LANES), row),
                  pl.BlockSpec((SSD_CONV, SSD_CONV_CH), const), pl.BlockSpec((1, SSD_CONV_CH), const),
                  pl.BlockSpec((1, LANES), const), pl.BlockSpec((1, LANES), const),
                  pl.BlockSpec((1, SSD_INNER), const), pl.BlockSpec((1, SSD_INNER), const),
                  pl.BlockSpec((LANES, SSD_INNER), const), pl.BlockSpec((ll, ll), const),
                  pl.BlockSpec((None, SSD_INNER, SSD_STATE), seq3),
                  pl.BlockSpec((None, SSD_CONV - 1, SSD_CONV_CH), seq3)],
        out_specs=(pl.BlockSpec((lv, SSD_INNER), row),
                   pl.BlockSpec((None, SSD_INNER, SSD_STATE), seq3),
                   pl.BlockSpec((None, SSD_CONV - 1, SSD_CONV_CH), seq3)),
        scratch_shapes=[pltpu.VMEM((SSD_STATE, SSD_INNER), f32), pltpu.VMEM((SUBLANES + ll, SSD_CONV_CH), f32)],
        compiler_params=_params(("arbitrary", "arbitrary")),
        name="ssd_mix",
    )(z, u, fdt, conv_w, conv_b.reshape(1, -1), pad(dt_bias), pad(a_log),
      jnp.repeat(d_skip, SSD_HEAD_DIM).reshape(1, -1), gnorm.reshape(1, -1), jnp.asarray(e, dtype=bf16), tri, h0, c0)


def _hgrn_kernel(q_ref, f_ref, i_ref, g_ref, la_ref, lb_ref, om_ref, gn_ref, tri_ref, s0_ref,
                 o_ref, sout_ref, st_ref, *, ll, lv):
    c = pl.program_id(1)

    @pl.when(c == 0)
    def _():
        for h in range(HEADS):
            st_ref[h] = s0_ref[h].T

    def padded(ref):
        x = ref[...]
        return x if lv == ll else jnp.concatenate([x, jnp.zeros((ll - lv, WIDTH), f32)], axis=0)

    fl = padded(f_ref)
    a = la_ref[...]
    b = lb_ref[...] + _log_sigmoid(fl)
    lf = jnp.maximum(a, b) + jnp.log(1.0 + jnp.exp(-jnp.abs(a - b)))
    key = om_ref[...] * jax.nn.sigmoid(-fl)
    q = _silu(padded(q_ref))
    v = padded(i_ref)
    if lv < ll:
        live = lax.broadcasted_iota(jnp.int32, (ll, WIDTH), 0) < lv
        lf = jnp.where(live, lf, 0.0)
        key = jnp.where(live, key, 0.0)
    cum = jnp.dot(tri_ref[...], lf, preferred_element_type=f32, precision=HIGHEST)
    gate = _silu(padded(g_ref))

    nsub = ll // HG_SUB
    lrow = lax.broadcasted_iota(jnp.int32, (HG_SUB, HEAD_DIM), 0)
    lane = lax.broadcasted_iota(jnp.int32, (HG_SUB, HG_SUB), 1)
    lrow2 = lax.broadcasted_iota(jnp.int32, (HG_SUB, HG_SUB), 0)
    for h in range(HEADS):
        sl = slice(h * HEAD_DIM, (h + 1) * HEAD_DIM)
        qh, kh, vh, ch = q[:, sl], key[:, sl], v[:, sl], cum[:, sl]
        vb = vh.astype(bf16)
        st = st_ref[h]
        c_last = ch[ll - 1:ll]
        o_rows = []
        for blk in range(nsub):
            r0 = blk * HG_SUB
            cb_, qb_, kb_ = ch[r0:r0 + HG_SUB], qh[r0:r0 + HG_SUB], kh[r0:r0 + HG_SUB]
            diag = jnp.zeros((HG_SUB, HG_SUB), f32)
            for s in range(HG_SUB):
                ok = lrow >= s
                e = jnp.exp(jnp.where(ok, cb_ - cb_[s:s + 1], 0.0))
                val = jnp.sum(jnp.where(ok, qb_ * kb_[s:s + 1] * e, 0.0), axis=-1, keepdims=True)
                diag = jnp.where(lane == s, val, diag)
            diag = jnp.where(lane <= lrow2, diag, 0.0)
            parts = []
            if r0 > 0:
                ref_c = ch[r0 - 1:r0]
                qt = (qb_ * jnp.exp(cb_ - ref_c)).astype(bf16)
                kt = (kh[:r0] * jnp.exp(ref_c - ch[:r0])).astype(bf16)
                parts.append(_dot_nt(qt, kt))
            parts.append(diag)
            if r0 + HG_SUB < ll:
                parts.append(jnp.zeros((HG_SUB, ll - r0 - HG_SUB), f32))
            att = jnp.concatenate(parts, axis=1) if len(parts) > 1 else parts[0]
            o_rows.append(_dot(att.astype(bf16), vb))
        oh = jnp.concatenate(o_rows, axis=0) + _dot_nt((qh * jnp.exp(ch)).astype(bf16), st.astype(bf16))
        st_ref[h] = st * jnp.exp(c_last) + _dot_tn(vb, (kh * jnp.exp(c_last - ch)).astype(bf16))
        ms = jnp.mean(oh * oh, axis=-1, keepdims=True)
        on = oh * lax.rsqrt(ms + NORM_EPS) * gn_ref[:, sl] * gate[:, sl]
        o_ref[:, sl] = on[:lv].astype(o_ref.dtype)

    @pl.when(c == pl.num_programs(1) - 1)
    def _():
        for h in range(HEADS):
            sout_ref[h] = st_ref[h].T


def hgrn_mix(proj, lower_bound, gnorm, s0, *, lv):
    m = proj.shape[0]
    nseq = s0.shape[0]
    ll = HG_CHUNK
    assert lv <= ll and lv % SUBLANES == 0 and m % (nseq * lv) == 0
    nchunk = m // (nseq * lv)
    lb = lower_bound.reshape(1, WIDTH).astype(f32)
    r = np.arange(ll)
    tri = jnp.asarray((r[:, None] >= r[None, :]).astype(np.float32))
    col = lambda k: pl.BlockSpec((lv, WIDTH), lambda i, c, k=k: (i * nchunk + c, k))
    const = lambda i, c: (0, 0)
    seq4 = lambda i, c: (i, 0, 0, 0)
    return pl.pallas_call(
        functools.partial(_hgrn_kernel, ll=ll, lv=lv),
        out_shape=(jax.ShapeDtypeStruct((m, WIDTH), bf16),
                   jax.ShapeDtypeStruct((nseq, HEADS, HEAD_DIM, HEAD_DIM), f32)),
        grid=(nseq, nchunk),
        in_specs=[col(0), col(1), col(2), col(3),
                  pl.BlockSpec((1, WIDTH), const), pl.BlockSpec((1, WIDTH), const), pl.BlockSpec((1, WIDTH), const),
                  pl.BlockSpec((1, WIDTH), const), pl.BlockSpec((ll, ll), const),
                  pl.BlockSpec((None, HEADS, HEAD_DIM, HEAD_DIM), seq4)],
        out_specs=(pl.BlockSpec((lv, WIDTH), lambda i, c: (i * nchunk + c, 0)),
                   pl.BlockSpec((None, HEADS, HEAD_DIM, HEAD_DIM), seq4)),
        scratch_shapes=[pltpu.VMEM((HEADS, HEAD_DIM, HEAD_DIM), f32)],
        compiler_params=_params(("arbitrary", "arbitrary")),
        name="hgrn_mix",
    )(proj, proj, proj, proj, jnp.log(lb), jnp.log1p(-lb), 1.0 - lb, gnorm.reshape(1, WIDTH), tri, s0)


def _heads_to_rows(x, nb, n_new):
    return x.reshape(nb, n_new, HEADS, HEAD_DIM).transpose(0, 2, 1, 3).reshape(nb, HEADS * n_new, HEAD_DIM)


def _rows_to_heads(x, nb, n_new):
    return x.reshape(nb, HEADS, n_new, HEAD_DIM).transpose(0, 2, 1, 3).reshape(nb * n_new, WIDTH)


def _token_head_rows(x, nb, n_new):
    return x.reshape(nb, n_new * HEADS, HEAD_DIM)


def _trunk(x, mem_k, mem_v, p, lower_bounds, *, nseq, decode):
    m = x.shape[0]
    t = m // nseq
    assert decode or nseq == 1
    tm = min(m, 1024)
    depth = p["norm_ffn1"].shape[0]
    st = {}
    for layer in range(depth):
        i = layer // 2
        hmid = norm_matmul(x, p["norm_ffn1"][layer], [(p["ffn1_gate"], layer, 0), (p["ffn1_up"], layer, 0)],
                           n=D_FF, tn=512, tm=tm, out_dtype=bf16, swiglu=True, name="ffn1_up")
        x = matmul_residual([hmid], p["ffn1_down"], layer, x, scale=0.5, tm=tm, tn=256, name="ffn1_down")
        gain = p["norm_mix"][layer]
        if layer % 2 == 0:
            w_in = p["ab_w_in"][i]
            qkv = norm_matmul(x, gain, [(w_in, None, 0)], n=3 * WIDTH, tn=1024, tm=tm, name="ab_qkv")
            off_z = 3 * WIDTH + HEADS
            off_x = off_z + SSD_INNER
            off_dt = off_x + SSD_CONV_CH
            w_fdt = jnp.concatenate([w_in[:, 3 * WIDTH:off_z], w_in[:, off_dt:off_dt + SSD_HEADS],
                                     jnp.zeros((D_MODEL, LANES - HEADS - SSD_HEADS), f32)], axis=1)
            zz = norm_matmul(x, gain, [(w_in[:, off_z:off_x], None, 0)], n=SSD_INNER, tn=512, tm=tm, name="ab_z")
            uu = norm_matmul(x, gain, [(w_in[:, off_x:off_dt], None, 0)], n=SSD_CONV_CH, tn=512, tm=tm, name="ab_xbc")
            fdt = norm_matmul(x, gain, [(w_fdt, None, 0)], n=LANES, tn=LANES, tm=tm, name="ab_fdt")
            logf, ck = fox_gate(fdt, p["ab_fox_fbias"][i], seg=t, tb=min(m, 512))
            k_new, v_new = qkv[:, WIDTH:2 * WIDTH], qkv[:, 2 * WIDTH:]
            if decode:
                bias_new = (-ck).T.reshape(nseq, 1, t * HEADS)
                o = decode_attention("fox", _heads_to_rows(qkv[:, :WIDTH], nseq, t), _token_head_rows(k_new, nseq, t),
                                     _token_head_rows(v_new, nseq, t), p["cache_fox_k"], p["cache_fox_v"],
                                     p["page_table"], bias_new=bias_new,
                                     bias_past=fox_past_bias(p["cache_fox_logf"], p["page_table"]))
                o_fox = _rows_to_heads(o, nseq, t).astype(bf16)
                h0, c0, lv = p["state_ssd"][i].reshape(nseq, SSD_INNER, SSD_STATE), p["state_ssd_conv"][i], t
            else:
                o_fox = fox_flash(qkv, ck, tq=1024, tk=1024, sub=512)
                h0 = jnp.zeros((nseq, SSD_INNER, SSD_STATE), f32)
                c0 = jnp.zeros((nseq, SSD_CONV - 1, SSD_CONV_CH), f32)
                lv = SSD_CHUNK
            y, h_new, c_new = ssd_mix(zz, uu, fdt, p["ab_conv_w"][i], p["ab_conv_b"][i], p["ab_dt_bias"][i],
                                      p["ab_A_log"][i], p["ab_D"][i], p["ab_ssd_norm"][i], h0, c0, lv=lv)
            x = matmul_residual([o_fox, y], p["ab_w_out"], i, x, scale=1.0, tm=tm, tn=512, name="ab_out")
            st.setdefault("fox_k", []).append(k_new.reshape(nseq, t, HEADS, HEAD_DIM))
            st.setdefault("fox_v", []).append(v_new.reshape(nseq, t, HEADS, HEAD_DIM))
            st.setdefault("fox_logf", []).append(logf.reshape(nseq, t, HEADS))
            st.setdefault("ssd_conv", []).append(c_new)
            st.setdefault("ssd_state", []).append(h_new.reshape(nseq, SSD_HEADS, SSD_HEAD_DIM, SSD_STATE))
        else:
            proj = norm_matmul(x, gain, [(p["cd_w_in"], i, 0)], n=7 * WIDTH, tn=1024, tm=tm, name="cd_in")
            k_new, v_new = proj[:, 5 * WIDTH:6 * WIDTH], proj[:, 6 * WIDTH:]
            if decode:
                s0, lv = p["state_hgrn"][i], t
                o = decode_attention("sb", _heads_to_rows(proj[:, 4 * WIDTH:5 * WIDTH], nseq, t),
                                     _token_head_rows(k_new, nseq, t), _token_head_rows(v_new, nseq, t),
                                     p["cache_sb_k"], p["cache_sb_v"], p["page_table"])
                o_sb = _rows_to_heads(o, nseq, t).astype(bf16)
            else:
                s0, lv = jnp.zeros((nseq, HEADS, HEAD_DIM, HEAD_DIM), f32), HG_CHUNK
                o_sb = sb_flash(proj, 4 * HEADS, tq=1024, tk=512, sub=1024)
            o_hg, s_new = hgrn_mix(proj, lower_bounds[layer], p["cd_hg_norm"][i], s0, lv=lv)
            x = matmul_residual([o_hg, o_sb], p["cd_w_out"], i, x, scale=1.0, tm=tm, tn=512, name="cd_out")
            st.setdefault("hgrn", []).append(s_new)
            st.setdefault("sb_k", []).append(k_new.reshape(nseq, t, HEADS, HEAD_DIM))
            st.setdefault("sb_v", []).append(v_new.reshape(nseq, t, HEADS, HEAD_DIM))
        qm = norm_matmul(x, p["norm_mem"][layer], [(p["mem_wq"], layer, 0)], n=MEM_WIDTH, tn=MEM_WIDTH, tm=tm,
                         name="mem_q")
        om = mem_attention(qm, mem_k, mem_v, layer, tq=min(t, 1024))
        x = matmul_residual([om], p["mem_wo"], layer, x, scale=1.0, tm=tm, tn=1024, name="mem_out")
        hmid = norm_matmul(x, p["norm_ffn2"][layer], [(p["ffn2_gate"], layer, 0), (p["ffn2_up"], layer, 0)],
                           n=D_FF, tn=512, tm=tm, out_dtype=bf16, swiglu=True, name="ffn2_up")
        x = matmul_residual([hmid], p["ffn2_down"], layer, x, scale=0.5, tm=tm, tn=256, name="ffn2_down")
    y = rmsnorm_rows(x, p["norm_final"], tm=min(m, 512))
    return y, {k: jnp.stack(v) for k, v in st.items()}


def kernel(x_prompt, x_sample, cache_fox_k, cache_fox_v, cache_fox_logf, state_ssd_conv, state_ssd, state_hgrn, cache_sb_k, cache_sb_v, cache_mem_k, cache_mem_v, page_table, mem_prompt, norm_ffn1, ffn1_gate, ffn1_up, ffn1_down, norm_mix, ab_w_in, ab_fox_fbias, ab_conv_w, ab_conv_b, ab_dt_bias, ab_A_log, ab_D, ab_ssd_norm, ab_w_out, cd_w_in, hg_lower_bound, cd_hg_norm, cd_w_out, norm_mem, norm_memkv, mem_wq, mem_wk, mem_wv, mem_wo, norm_ffn2, ffn2_gate, ffn2_up, ffn2_down, norm_final):
    p = dict(norm_ffn1=norm_ffn1, ffn1_gate=ffn1_gate, ffn1_up=ffn1_up, ffn1_down=ffn1_down, norm_mix=norm_mix,
             ab_w_in=ab_w_in, ab_fox_fbias=ab_fox_fbias, ab_conv_w=ab_conv_w, ab_conv_b=ab_conv_b,
             ab_dt_bias=ab_dt_bias, ab_A_log=ab_A_log, ab_D=ab_D, ab_ssd_norm=ab_ssd_norm, ab_w_out=ab_w_out,
             cd_w_in=cd_w_in, cd_hg_norm=cd_hg_norm, cd_w_out=cd_w_out, norm_mem=norm_mem, mem_wq=mem_wq,
             mem_wo=mem_wo, norm_ffn2=norm_ffn2, ffn2_gate=ffn2_gate, ffn2_up=ffn2_up, ffn2_down=ffn2_down,
             norm_final=norm_final, cache_fox_k=cache_fox_k, cache_fox_v=cache_fox_v, cache_fox_logf=cache_fox_logf,
             cache_sb_k=cache_sb_k, cache_sb_v=cache_sb_v, page_table=page_table, state_ssd=state_ssd,
             state_ssd_conv=state_ssd_conv, state_hgrn=state_hgrn)
    depth = norm_ffn1.shape[0]
    for name in ("ffn1_down", "ffn2_down", "ab_w_in", "ab_w_out", "cd_w_in", "cd_w_out"):
        p[name] = p[name].astype(bf16)
    probs = jax.nn.softmax(hg_lower_bound.astype(f32), axis=0)
    lower_bounds = jnp.cumsum(probs, axis=0) - probs[0]

    bp, seq, _ = x_prompt.shape
    bs, dseq, _ = x_sample.shape
    mem_len = mem_prompt.shape[1]
    mem_rows = mem_prompt.reshape(bp * mem_len, D_MODEL)
    mk = [norm_matmul(mem_rows, norm_memkv[l], [(mem_wk, l, 0)], n=MEM_WIDTH, tn=MEM_WIDTH, tm=mem_len, name="mem_k")
          for l in range(depth)]
    mv = [norm_matmul(mem_rows, norm_memkv[l], [(mem_wv, l, 0)], n=MEM_WIDTH, tn=MEM_WIDTH, tm=mem_len, name="mem_v")
          for l in range(depth)]
    mem_k_p = jnp.stack(mk).reshape(depth, bp, mem_len, MEM_WIDTH)
    mem_v_p = jnp.stack(mv).reshape(depth, bp, mem_len, MEM_WIDTH)

    y_p, sp = _trunk(x_prompt.reshape(bp * seq, D_MODEL), mem_k_p, mem_v_p, p, lower_bounds, nseq=bp, decode=False)
    y_s, ss = _trunk(x_sample.reshape(bs * dseq, D_MODEL), cache_mem_k.reshape(depth, bs, mem_len, MEM_WIDTH),
                     cache_mem_v.reshape(depth, bs, mem_len, MEM_WIDTH), p, lower_bounds, nseq=bs, decode=True)

    m5 = (depth, bp, mem_len, MEM_HEADS, HEAD_DIM)
    return (y_p.reshape(bp, seq, D_MODEL), y_s.reshape(bs, dseq, D_MODEL),
            sp["fox_k"], sp["fox_v"], sp["fox_logf"], sp["ssd_conv"], sp["ssd_state"], sp["hgrn"], sp["sb_k"], sp["sb_v"],
            mem_k_p.reshape(m5), mem_v_p.reshape(m5),
            ss["fox_k"], ss["fox_v"], ss["fox_logf"], ss["ssd_conv"], ss["ssd_state"], ss["hgrn"], ss["sb_k"], ss["sb_v"])
```
